```python
import math
import jax, jax.numpy as jnp
from jax import lax
import numpy as np

D_MODEL = 2048
BATCH = 2
SEQ = 16384
DEPTH = 2

N_A_LAYERS = max(DEPTH // 2, 1)
N_B_LAYERS = DEPTH - N_A_LAYERS

DEEPNORM_ALPHA = (2.0 * DEPTH) ** 0.25
DEEPNORM_BETA = (8.0 * DEPTH) ** -0.25
LN_EPS = 1e-5

D_FF = 5632
FFN_HALF = 0.5

GDN_QK_HEADS = 16
GDN_V_HEADS = 32
GDN_HEAD_K = 128
GDN_HEAD_V = 128
GDN_CONV = 4
GDN_CHUNK = 64
GDN_QK_DIM = GDN_QK_HEADS * GDN_HEAD_K
GDN_V_DIM = GDN_V_HEADS * GDN_HEAD_V
GDN_CONV_DIM = 2 * GDN_QK_DIM + GDN_V_DIM
GDN_IN_DIM = GDN_CONV_DIM + GDN_V_DIM + 2 * GDN_V_HEADS
GDN_NORM_EPS = 1e-6

MOBA_Q_HEADS = 16
MOBA_KV_HEADS = 4
MOBA_HEAD_DIM = 128
MOBA_GROUP = MOBA_Q_HEADS // MOBA_KV_HEADS
MOBA_KV_DIM = MOBA_KV_HEADS * MOBA_HEAD_DIM
MOBA_BLOCK = 256
MOBA_TOPK = 3
MOBA_Q_CHUNK = 32

ROPE_THETA = 500000.0
ROPE_DIM = MOBA_HEAD_DIM // 4

kernel_name = "yoco_gdn_moba_macaron_deepnorm"


def layer_norm(x, g, b):
    xf = x.astype(jnp.float32)
    mu = jnp.mean(xf, -1, keepdims=True)
    var = jnp.mean(jnp.square(xf - mu), -1, keepdims=True)
    return ((xf - mu) * lax.rsqrt(var + LN_EPS) * g.astype(jnp.float32) + b.astype(jnp.float32)).astype(x.dtype)


def deepnorm_residual(h, y, g, b):
    return layer_norm(DEEPNORM_ALPHA * h + y, g, b)


def swiglu(x, w_in, w_out):
    gu = jnp.einsum('bsd,dtf->bstf', x, w_in)
    return (jax.nn.silu(gu[..., 0, :]) * gu[..., 1, :]) @ w_out


def causal_dwconv(x, w):
    k = w.shape[0]
    return lax.conv_general_dilated(x, w[:, None, :].astype(x.dtype), window_strides=(1,),
                                    padding=[(k - 1, 0)],
                                    dimension_numbers=('NWC', 'WIO', 'NWC'),
                                    feature_group_count=x.shape[-1])


def l2norm(x):
    xf = x.astype(jnp.float32)
    return xf * lax.rsqrt(jnp.sum(xf * xf, -1, keepdims=True) + GDN_NORM_EPS)


def gated_delta_rule(q, k, v, g, beta):
    B, H, S, dk = q.shape
    dv = v.shape[-1]
    C = GDN_CHUNK
    N = S // C
    q = q * (dk ** -0.5)

    def chunks(t):
        return t.reshape(B, H, N, C, *t.shape[3:])

    q, k, v, g, beta = chunks(q), chunks(k), chunks(v), chunks(g), chunks(beta)
    g = jnp.cumsum(g, axis=-1)
    tri = jnp.tril(jnp.ones((C, C), bool))
    strict = jnp.tril(jnp.ones((C, C), bool), -1)
    gdiff = g[..., :, None] - g[..., None, :]
    decay = jnp.where(tri, jnp.exp(jnp.where(tri, gdiff, 0.0)), 0.0)

    kb = k * beta[..., None]
    L = jnp.where(strict, jnp.einsum('bhnid,bhnjd->bhnij', kb, k) * decay, 0.0)
    A = L + jnp.eye(C, dtype=L.dtype)
    rhs = jnp.concatenate([v * beta[..., None], kb * jnp.exp(g)[..., None]], axis=-1)
    sol = lax.linalg.triangular_solve(A, rhs, left_side=True, lower=True, unit_diagonal=True)
    u, w = sol[..., :dv], sol[..., dv:]
    attn = jnp.where(tri, jnp.einsum('bhnid,bhnjd->bhnij', q, k) * decay, 0.0)
    g_last = g[..., -1]
    k_to_end = k * jnp.exp(g_last[..., None] - g)[..., None]
    q_decay = q * jnp.exp(g)[..., None]

    xs = tuple(jnp.moveaxis(t, 2, 0) for t in (q_decay, k_to_end, u, w, attn, g_last))

    def step(state, inp):
        qd, ke, uc, wc, at, gl = inp
        v_new = uc - jnp.einsum('bhck,bhkv->bhcv', wc, state)
        o = jnp.einsum('bhck,bhkv->bhcv', qd, state) + jnp.einsum('bhij,bhjv->bhiv', at, v_new)
        state = state * jnp.exp(gl)[..., None, None] + jnp.einsum('bhck,bhcv->bhkv', ke, v_new)
        return state, o

    state0 = jnp.zeros((B, H, dk, dv), jnp.float32)
    _, o = lax.scan(step, state0, xs)
    return jnp.moveaxis(o, 0, 2).reshape(B, H, S, dv)


def gated_deltanet(x, w_in, conv_w, a_log, dt_bias, norm_w, w_out):
    B, S, _ = x.shape
    proj = x @ w_in
    qkv = jax.nn.silu(causal_dwconv(proj[..., :GDN_CONV_DIM], conv_w))
    z = proj[..., GDN_CONV_DIM:GDN_CONV_DIM + GDN_V_DIM]
    b_logit = proj[..., GDN_CONV_DIM + GDN_V_DIM:GDN_CONV_DIM + GDN_V_DIM + GDN_V_HEADS]
    a_logit = proj[..., GDN_CONV_DIM + GDN_V_DIM + GDN_V_HEADS:]
    q = l2norm(qkv[..., :GDN_QK_DIM].reshape(B, S, GDN_QK_HEADS, GDN_HEAD_K))
    k = l2norm(qkv[..., GDN_QK_DIM:2 * GDN_QK_DIM].reshape(B, S, GDN_QK_HEADS, GDN_HEAD_K))
    v = qkv[..., 2 * GDN_QK_DIM:].reshape(B, S, GDN_V_HEADS, GDN_HEAD_V).astype(jnp.float32)
    rep = GDN_V_HEADS // GDN_QK_HEADS
    q = jnp.repeat(q, rep, axis=2)
    k = jnp.repeat(k, rep, axis=2)
    beta = jax.nn.sigmoid(b_logit.astype(jnp.float32))
    g = -jnp.exp(a_log.astype(jnp.float32)) * jax.nn.softplus(a_logit.astype(jnp.float32) + dt_bias.astype(jnp.float32))
    o = gated_delta_rule(q.transpose(0, 2, 1, 3), k.transpose(0, 2, 1, 3), v.transpose(0, 2, 1, 3),
                         g.transpose(0, 2, 1), beta.transpose(0, 2, 1))
    o = o.transpose(0, 2, 1, 3)
    o = o * lax.rsqrt(jnp.mean(o * o, -1, keepdims=True) + GDN_NORM_EPS) * norm_w.astype(jnp.float32)
    o = o * jax.nn.silu(z.reshape(B, S, GDN_V_HEADS, GDN_HEAD_V).astype(jnp.float32))
    return o.reshape(B, S, GDN_V_DIM).astype(x.dtype) @ w_out


def rope_tables(S):
    pos = jnp.arange(S, dtype=jnp.float32)
    inv_freq = ROPE_THETA ** (-jnp.arange(0, ROPE_DIM, 2, dtype=jnp.float32) / ROPE_DIM)
    ang = pos[:, None] * inv_freq[None, :]
    return jnp.cos(ang), jnp.sin(ang)


def partial_rope(x, cos, sin):
    half = ROPE_DIM // 2
    x1 = x[..., :half].astype(jnp.float32)
    x2 = x[..., half:ROPE_DIM].astype(jnp.float32)
    rot = jnp.concatenate([x1 * cos - x2 * sin, x2 * cos + x1 * sin], -1).astype(x.dtype)
    return jnp.concatenate([rot, x[..., ROPE_DIM:]], -1)


def shared_kv(h, w_kv, cos, sin):
    B, S, _ = h.shape
    kv = h @ w_kv
    k = kv[..., :MOBA_KV_DIM].reshape(B, S, MOBA_KV_HEADS, MOBA_HEAD_DIM).transpose(0, 2, 1, 3)
    v = kv[..., MOBA_KV_DIM:].reshape(B, S, MOBA_KV_HEADS, MOBA_HEAD_DIM).transpose(0, 2, 1, 3)
    k = partial_rope(k, cos, sin)
    n_blocks = max(-(-S // MOBA_BLOCK), MOBA_TOPK)
    pad = n_blocks * MOBA_BLOCK - S
    k = jnp.pad(k, ((0, 0), (0, 0), (0, pad), (0, 0)))
    v = jnp.pad(v, ((0, 0), (0, 0), (0, pad), (0, 0)))
    k_blocks = k.reshape(B, MOBA_KV_HEADS, n_blocks, MOBA_BLOCK, MOBA_HEAD_DIM)
    v_blocks = v.reshape(B, MOBA_KV_HEADS, n_blocks, MOBA_BLOCK, MOBA_HEAD_DIM)
    k_means = jnp.mean(k_blocks.astype(jnp.float32), axis=3)
    return k_blocks, v_blocks, k_means


def moba_attention(h, w_q, w_o, k_blocks, v_blocks, k_means, cos, sin):
    B, S, _ = h.shape
    HQ, HKV, G, dh, QC, BLK = MOBA_Q_HEADS, MOBA_KV_HEADS, MOBA_GROUP, MOBA_HEAD_DIM, MOBA_Q_CHUNK, MOBA_BLOCK
    n_blocks = k_blocks.shape[2]
    q = (h @ w_q).reshape(B, S, HQ, dh).transpose(0, 2, 1, 3)
    q = partial_rope(q, cos, sin) * (dh ** -0.5)
    NQ = S // QC
    q_chunks = q.reshape(B, HQ, NQ, QC, dh).transpose(2, 0, 1, 3, 4)
    kv_head = jnp.arange(HQ) // G
    k_means_q = k_means[:, kv_head]
    b_idx = jnp.arange(B)[:, None, None, None]
    h_idx = kv_head[None, :, None, None]
    block_ids = jnp.arange(n_blocks)
    slot_ids = jnp.arange(MOBA_TOPK)

    def one_chunk(args):
        qc, ci = args
        start = ci * QC
        cur = start // BLK
        qpos = start + jnp.arange(QC)
        gate = jnp.einsum('bhqd,bhnd->bhqn', qc.astype(jnp.float32), k_means_q)
        gate = jnp.where(block_ids < cur, gate, -jnp.inf)
        _, sel = lax.top_k(gate, MOBA_TOPK)
        valid = slot_ids < cur
        ks = k_blocks[b_idx, h_idx, sel]
        vs = v_blocks[b_idx, h_idx, sel]
        s_sel = jnp.einsum('bhqd,bhqtkd->bhqtk', qc, ks).astype(jnp.float32)
        s_sel = jnp.where(valid[:, None], s_sel, -jnp.inf)
        k_own = lax.dynamic_index_in_dim(k_blocks, cur, axis=2, keepdims=False)
        v_own = lax.dynamic_index_in_dim(v_blocks, cur, axis=2, keepdims=False)
        s_own = jnp.einsum('bhgqd,bhkd->bhgqk', qc.reshape(B, HKV, G, QC, dh), k_own)
        s_own = s_own.reshape(B, HQ, QC, BLK).astype(jnp.float32)
        kpos = cur * BLK + jnp.arange(BLK)
        s_own = jnp.where(kpos[None, :] <= qpos[:, None], s_own, -jnp.inf)
        p = jax.nn.softmax(jnp.concatenate([s_own, s_sel.reshape(B, HQ, QC, MOBA_TOPK * BLK)], -1), axis=-1)
        p_own = p[..., :BLK].reshape(B, HKV, G, QC, BLK)
        p_sel = p[..., BLK:].reshape(B, HQ, QC, MOBA_TOPK, BLK)
        o = jnp.einsum('bhgqk,bhkd->bhgqd', p_own, v_own.astype(jnp.float32)).reshape(B, HQ, QC, dh)
        o = o + jnp.einsum('bhqtk,bhqtkd->bhqd', p_sel, vs.astype(jnp.float32))
        return o.astype(h.dtype)

    o = lax.map(one_chunk, (q_chunks, jnp.arange(NQ)))
    o = o.transpose(1, 0, 3, 2, 4).reshape(B, S, HQ * dh)
    return o @ w_o


def setup_inputs(seed: int = 0) -> dict:
    key = jax.random.key(seed)
    ks = jax.random.split(key, 16)
    f32 = jnp.float32

    def dense(k, shape, fan_in, scale=1.0):
        return jax.random.normal(k, shape, f32) * (fan_in ** -0.5) * scale

    x = jax.random.normal(ks[0], (BATCH, SEQ, D_MODEL), f32)
    ln_g = 1.0 + 0.02 * jax.random.normal(ks[1], (DEPTH, 3, D_MODEL), f32)
    ln_b = 0.02 * jax.random.normal(ks[2], (DEPTH, 3, D_MODEL), f32)
    w_ffn_in = dense(ks[3], (DEPTH, 2, D_MODEL, 2, D_FF), D_MODEL)
    w_ffn_out = dense(ks[4], (DEPTH, 2, D_FF, D_MODEL), D_FF, DEEPNORM_BETA)
    gdn_w_in = dense(ks[5], (N_A_LAYERS, D_MODEL, GDN_IN_DIM), D_MODEL)
    gdn_conv_w = dense(ks[6], (N_A_LAYERS, GDN_CONV, GDN_CONV_DIM), GDN_CONV)
    gdn_a_log = jnp.log(jax.random.uniform(ks[7], (N_A_LAYERS, GDN_V_HEADS), f32, 1.0, 16.0))
    dt = jnp.exp(jax.random.uniform(ks[8], (N_A_LAYERS, GDN_V_HEADS), f32, math.log(1e-3), math.log(1e-1)))
    gdn_dt_bias = dt + jnp.log(-jnp.expm1(-dt))
    gdn_norm_w = 1.0 + 0.02 * jax.random.normal(ks[9], (N_A_LAYERS, GDN_HEAD_V), f32)
    gdn_w_out = dense(ks[10], (N_A_LAYERS, GDN_V_DIM, D_MODEL), GDN_V_DIM, DEEPNORM_BETA)
    moba_w_kv = dense(ks[11], (D_MODEL, 2 * MOBA_KV_DIM), D_MODEL)
    moba_w_q = dense(ks[12], (N_B_LAYERS, D_MODEL, MOBA_Q_HEADS * MOBA_HEAD_DIM), D_MODEL)
    moba_w_out = dense(ks[13], (N_B_LAYERS, MOBA_Q_HEADS * MOBA_HEAD_DIM, D_MODEL),
                       MOBA_Q_HEADS * MOBA_HEAD_DIM, DEEPNORM_BETA)
    return {"x": x, "ln_g": ln_g, "ln_b": ln_b, "w_ffn_in": w_ffn_in, "w_ffn_out": w_ffn_out,
            "gdn_w_in": gdn_w_in, "gdn_conv_w": gdn_conv_w, "gdn_a_log": gdn_a_log,
            "gdn_dt_bias": gdn_dt_bias, "gdn_norm_w": gdn_norm_w, "gdn_w_out": gdn_w_out,
            "moba_w_kv": moba_w_kv, "moba_w_q": moba_w_q, "moba_w_out": moba_w_out}


def reference(x, ln_g, ln_b, w_ffn_in, w_ffn_out, gdn_w_in, gdn_conv_w, gdn_a_log, gdn_dt_bias,
              gdn_norm_w, gdn_w_out, moba_w_kv, moba_w_q, moba_w_out):
    S = x.shape[1]
    cos, sin = rope_tables(S)
    h = x
    kv = None
    for layer in range(DEPTH):
        h = deepnorm_residual(h, FFN_HALF * swiglu(h, w_ffn_in[layer, 0], w_ffn_out[layer, 0]),
                              ln_g[layer, 0], ln_b[layer, 0])
        if layer < N_A_LAYERS:
            mix = gated_deltanet(h, gdn_w_in[layer], gdn_conv_w[layer], gdn_a_log[layer],
                                 gdn_dt_bias[layer], gdn_norm_w[layer], gdn_w_out[layer])
        else:
            kb, vb, km = kv
            j = layer - N_A_LAYERS
            mix = moba_attention(h, moba_w_q[j], moba_w_out[j], kb, vb, km, cos, sin)
        h = deepnorm_residual(h, mix, ln_g[layer, 1], ln_b[layer, 1])
        h = deepnorm_residual(h, FFN_HALF * swiglu(h, w_ffn_in[layer, 1], w_ffn_out[layer, 1]),
                              ln_g[layer, 2], ln_b[layer, 2])
        if layer == N_A_LAYERS - 1:
            kv = shared_kv(h, moba_w_kv, cos, sin)
    return h
```

```python
import functools
import math

import jax
import jax.numpy as jnp
from jax import lax
from jax.experimental import pallas as pl
from jax.experimental.pallas import tpu as pltpu

F32 = jnp.float32
BF16 = jnp.bfloat16

DEPTH = 2
DEEPNORM_ALPHA = (2.0 * DEPTH) ** 0.25
LN_EPS = 1e-5
FFN_HALF = 0.5

GDN_QK_HEADS = 16
GDN_V_HEADS = 32
GDN_HEAD = 128
GDN_CONV = 4
GDN_CHUNK = 64
GDN_QK_DIM = GDN_QK_HEADS * GDN_HEAD
GDN_V_DIM = GDN_V_HEADS * GDN_HEAD
GDN_CONV_DIM = 2 * GDN_QK_DIM + GDN_V_DIM
GDN_NORM_EPS = 1e-6

MOBA_Q_HEADS = 16
MOBA_KV_HEADS = 4
MOBA_HEAD = 128
MOBA_GROUP = MOBA_Q_HEADS // MOBA_KV_HEADS
MOBA_KV_DIM = MOBA_KV_HEADS * MOBA_HEAD
MOBA_BLOCK = 256
MOBA_TOPK = 3
ROPE_THETA = 500000.0
ROPE_DIM = MOBA_HEAD // 4

LANES = 128
VMEM_LIMIT = 56 * 1024 * 1024
MASK_NEG = -1e30

NT_DIMS = (((1,), (1,)), ((), ()))
TN_DIMS = (((0,), (0,)), ((), ()))


def _params(*sem):
    return pltpu.CompilerParams(dimension_semantics=sem, vmem_limit_bytes=VMEM_LIMIT)


def _layer_norm(y, g, b):
    mu = jnp.mean(y, axis=-1, keepdims=True)
    yc = y - mu
    var = jnp.mean(yc * yc, axis=-1, keepdims=True)
    return yc * lax.rsqrt(var + LN_EPS) * g + b


def _silu(x):
    return x * jax.nn.sigmoid(x)


def _ffn_ln_kernel(x_ref, wg_ref, wu_ref, wo_ref, g_ref, b_ref, o_ref, xb_ref, acc_ref):
    f = pl.program_id(1)

    @pl.when(f == 0)
    def _():
        xb_ref[...] = x_ref[...].astype(BF16)
        acc_ref[...] = jnp.zeros_like(acc_ref)

    xb = xb_ref[...]
    gate = jnp.dot(xb, wg_ref[...], preferred_element_type=F32)
    up = jnp.dot(xb, wu_ref[...], preferred_element_type=F32)
    mid = (_silu(gate) * up).astype(BF16)
    acc_ref[...] += jnp.dot(mid, wo_ref[...], preferred_element_type=F32)

    @pl.when(f == pl.num_programs(1) - 1)
    def _():
        y = DEEPNORM_ALPHA * x_ref[...] + FFN_HALF * acc_ref[...]
        o_ref[...] = _layer_norm(y, g_ref[...], b_ref[...])


def _ffn_ln(h, wg, wu, wo, g, b, *, tm=512, tf=512):
    t, d = h.shape
    f = wg.shape[1]
    return pl.pallas_call(
        _ffn_ln_kernel,
        grid=(t // tm, f // tf),
        in_specs=[
            pl.BlockSpec((tm, d), lambda i, j: (i, 0)),
            pl.BlockSpec((d, tf), lambda i, j: (0, j)),
            pl.BlockSpec((d, tf), lambda i, j: (0, j)),
            pl.BlockSpec((tf, d), lambda i, j: (j, 0)),
            pl.BlockSpec((1, d), lambda i, j: (0, 0)),
            pl.BlockSpec((1, d), lambda i, j: (0, 0)),
        ],
        out_specs=pl.BlockSpec((tm, d), lambda i, j: (i, 0)),
        out_shape=jax.ShapeDtypeStruct((t, d), F32),
        scratch_shapes=[pltpu.VMEM((tm, d), BF16), pltpu.VMEM((tm, d), F32)],
        compiler_params=_params("parallel", "arbitrary"),
        name="ffn_ln",
    )(h, wg, wu, wo, g, b)


def _mm_kernel(x_ref, w_ref, o_ref, xb_ref):
    @pl.when(pl.program_id(1) == 0)
    def _():
        xb_ref[...] = x_ref[...].astype(BF16)

    o_ref[...] = jnp.dot(xb_ref[...], w_ref[...], preferred_element_type=F32).astype(o_ref.dtype)


def _mm(x, w, *, tm=512, tn=1024, out_dtype=F32):
    t, k = x.shape
    n = w.shape[1]
    tn = min(tn, n)
    return pl.pallas_call(
        _mm_kernel,
        grid=(t // tm, n // tn),
        in_specs=[pl.BlockSpec((tm, k), lambda i, j: (i, 0)),
                  pl.BlockSpec((k, tn), lambda i, j: (0, j))],
        out_specs=pl.BlockSpec((tm, tn), lambda i, j: (i, j)),
        out_shape=jax.ShapeDtypeStruct((t, n), out_dtype),
        scratch_shapes=[pltpu.VMEM((tm, k), BF16)],
        compiler_params=_params("parallel", "arbitrary"),
        name="proj",
    )(x, w)


def _mm_res_ln_kernel(x_ref, w_ref, h_ref, g_ref, b_ref, o_ref, acc_ref):
    k = pl.program_id(1)

    @pl.when(k == 0)
    def _():
        acc_ref[...] = jnp.zeros_like(acc_ref)

    acc_ref[...] += jnp.dot(x_ref[...], w_ref[...], preferred_element_type=F32)

    @pl.when(k == pl.num_programs(1) - 1)
    def _():
        y = DEEPNORM_ALPHA * h_ref[...] + acc_ref[...]
        o_ref[...] = _layer_norm(y, g_ref[...], b_ref[...])


def _mm_res_ln(x, w, h, g, b, *, tm=512, tk=1024):
    t, kdim = x.shape
    d = w.shape[1]
    return pl.pallas_call(
        _mm_res_ln_kernel,
        grid=(t // tm, kdim // tk),
        in_specs=[
            pl.BlockSpec((tm, tk), lambda i, k: (i, k)),
            pl.BlockSpec((tk, d), lambda i, k: (k, 0)),
            pl.BlockSpec((tm, d), lambda i, k: (i, 0)),
            pl.BlockSpec((1, d), lambda i, k: (0, 0)),
            pl.BlockSpec((1, d), lambda i, k: (0, 0)),
        ],
        out_specs=pl.BlockSpec((tm, d), lambda i, k: (i, 0)),
        out_shape=jax.ShapeDtypeStruct((t, d), F32),
        scratch_shapes=[pltpu.VMEM((tm, d), F32)],
        compiler_params=_params("parallel", "arbitrary"),
        name="proj_res_ln",
    )(x, w, h, g, b)


def _gdn_gate_kernel(x_ref, w_ref, alog_ref, dtb_ref, o_ref):
    tm = x_ref.shape[0]
    logits = jnp.dot(x_ref[...].astype(BF16), w_ref[...], preferred_element_type=F32)
    lane = lax.broadcasted_iota(jnp.int32, (tm, LANES), 1)
    row = lax.broadcasted_iota(jnp.int32, (tm, LANES), 0) % GDN_CHUNK
    beta = jax.nn.sigmoid(logits)
    z = logits + dtb_ref[...]
    softplus = jnp.maximum(z, 0.0) + jnp.log1p(jnp.exp(-jnp.abs(z)))
    c = -jnp.exp(alog_ref[...]) * softplus
    shift = 1
    while shift < GDN_CHUNK:
        c = c + jnp.where(row >= shift, pltpu.roll(c, shift, axis=0), 0.0)
        shift *= 2
    o_ref[...] = jnp.where(lane < GDN_V_HEADS, beta, jnp.where(lane < 2 * GDN_V_HEADS, c, 0.0))


def _gdn_gates(h, w_ba, alog, dtb, *, tm=512):
    t, d = h.shape
    return pl.pallas_call(
        _gdn_gate_kernel,
        grid=(t // tm,),
        in_specs=[pl.BlockSpec((tm, d), lambda i: (i, 0)),
                  pl.BlockSpec((d, LANES), lambda i: (0, 0)),
                  pl.BlockSpec((1, LANES), lambda i: (0, 0)),
                  pl.BlockSpec((1, LANES), lambda i: (0, 0))],
        out_specs=pl.BlockSpec((tm, LANES), lambda i: (i, 0)),
        out_shape=jax.ShapeDtypeStruct((t, LANES), F32),
        compiler_params=_params("parallel"),
        name="gdn_gates",
    )(h, w_ba, alog, dtb)


def _gdn_conv_kernel(x_ref, halo_ref, w_ref, o_ref, buf_ref, *, tiles_per_seq, qk_blocks):
    i = pl.program_id(0)
    j = pl.program_id(1)
    tm, tc = x_ref.shape
    halo = jnp.where(i % tiles_per_seq == 0, 0.0, halo_ref[...])
    buf_ref[0:8, :] = halo
    buf_ref[8:, :] = x_ref[...]
    w = w_ref[...]
    y = x_ref[...] * w[GDN_CONV - 1:GDN_CONV, :]
    for back in range(1, GDN_CONV):
        y = y + buf_ref[pl.ds(8 - back, tm), :] * w[GDN_CONV - 1 - back:GDN_CONV - back, :]
    y = _silu(y)

    @pl.when(j < qk_blocks)
    def _():
        for hh in range(tc // LANES):
            seg = y[:, hh * LANES:(hh + 1) * LANES]
            ss = jnp.sum(seg * seg, axis=-1, keepdims=True)
            o_ref[:, hh * LANES:(hh + 1) * LANES] = seg * lax.rsqrt(ss + GDN_NORM_EPS)

    @pl.when(j >= qk_blocks)
    def _():
        o_ref[...] = y


def _gdn_conv(x, w, seq, *, tm=512, tc=512):
    t, c = x.shape
    kern = functools.partial(_gdn_conv_kernel, tiles_per_seq=seq // tm,
                             qk_blocks=2 * GDN_QK_DIM // tc)
    return pl.pallas_call(
        kern,
        grid=(t // tm, c // tc),
        in_specs=[pl.BlockSpec((tm, tc), lambda i, j: (i, j)),
                  pl.BlockSpec((8, tc), lambda i, j: (jnp.maximum(i * (tm // 8) - 1, 0), j)),
                  pl.BlockSpec((GDN_CONV, tc), lambda i, j: (0, j))],
        out_specs=pl.BlockSpec((tm, tc), lambda i, j: (i, j)),
        out_shape=jax.ShapeDtypeStruct((t, c), F32),
        scratch_shapes=[pltpu.VMEM((tm + 8, tc), F32)],
        compiler_params=_params("parallel", "parallel"),
        name="gdn_conv",
    )(x, x, w)


def _lane_pick(x, lane_idx, lane):
    return jnp.sum(jnp.where(lane_idx == lane, x, 0.0), axis=1, keepdims=True)


def _split3(x):
    hi = x.astype(BF16)
    r1 = x - hi.astype(F32)
    mid = r1.astype(BF16)
    lo = (r1 - mid.astype(F32)).astype(BF16)
    return hi.astype(F32), mid.astype(F32), lo.astype(F32)


def _gdn_prep_kernel(q_ref, k_ref, v_ref, gb_ref, u_ref, w_ref, qd_ref, ke_ref, at_ref):
    hq = pl.program_id(1)
    tt = q_ref.shape[0]
    c = GDN_CHUNK
    scale = GDN_HEAD ** -0.5
    lane = lax.broadcasted_iota(jnp.int32, (c, LANES), 1)
    ii = lax.broadcasted_iota(jnp.int32, (c, c), 0)
    jj = lax.broadcasted_iota(jnp.int32, (c, c), 1)
    lower = ii >= jj
    strict = ii > jj
    for ch in range(tt // c):
        rows = pl.ds(ch * c, c)
        kc = k_ref[rows, :]
        qc = q_ref[rows, :] * scale
        gbc = gb_ref[rows, :]
        kbf = kc.astype(BF16)
        kk = lax.dot_general(kbf, kbf, NT_DIMS, preferred_element_type=F32)
        qk = lax.dot_general(qc.astype(BF16), kbf, NT_DIMS, preferred_element_type=F32)
        for hv in range(2):
            head = 2 * hq + hv
            beta = _lane_pick(gbc, lane, head)
            gc = _lane_pick(gbc, lane, GDN_V_HEADS + head)
            hi, mid, lo = _split3(gc)
            a_mat = jnp.where(lane == 0, hi, jnp.where(lane == 1, mid, jnp.where(
                lane == 2, lo, jnp.where(lane < 6, 1.0, 0.0)))).astype(BF16)
            b_mat = jnp.where(lane < 3, 1.0, jnp.where(lane == 3, -hi, jnp.where(
                lane == 4, -mid, jnp.where(lane == 5, -lo, 0.0)))).astype(BF16)
            gdiff = lax.dot_general(a_mat, b_mat, NT_DIMS, preferred_element_type=F32)
            decay = jnp.where(lower, jnp.exp(jnp.where(lower, gdiff, 0.0)), 0.0)
            lmat = jnp.where(strict, kk * beta * decay, 0.0)
            attn = jnp.where(lower, qk * decay, 0.0)
            eg = jnp.exp(gc)
            vc = v_ref[rows, hv * GDN_HEAD:(hv + 1) * GDN_HEAD]
            kb = kc * beta
            x = jnp.concatenate([vc * beta, kb * eg], axis=1)
            p = -lmat
            n_fact = int(math.log2(c))
            for t in range(n_fact):
                pb = p.astype(BF16)
                x = x + jnp.dot(pb, x.astype(BF16), preferred_element_type=F32)
                if t + 1 < n_fact:
                    p = jnp.dot(pb, pb, preferred_element_type=F32)
            g_last = gc[c - 1:c, :]
            cols = pl.ds(hv * GDN_HEAD, GDN_HEAD)
            u_ref[rows, cols] = x[:, :GDN_HEAD]
            w_ref[rows, cols] = x[:, GDN_HEAD:].astype(BF16)
            qd_ref[rows, cols] = (qc * eg).astype(BF16)
            ke_ref[rows, cols] = (kc * jnp.exp(g_last - gc)).astype(BF16)
            at_ref[rows, pl.ds(hv * GDN_HEAD, c)] = attn.astype(BF16)


def _gdn_prep(qkv, gb, *, tt=256):
    t = qkv.shape[0]
    vd = GDN_V_DIM
    pair = 2 * GDN_HEAD
    q_blk0 = 0
    k_blk0 = GDN_QK_DIM // GDN_HEAD
    v_blk0 = 2 * GDN_QK_DIM // pair
    out_spec = pl.BlockSpec((tt, pair), lambda i, h: (i, h))
    return pl.pallas_call(
        _gdn_prep_kernel,
        grid=(t // tt, GDN_QK_HEADS),
        in_specs=[pl.BlockSpec((tt, GDN_HEAD), lambda i, h: (i, q_blk0 + h)),
                  pl.BlockSpec((tt, GDN_HEAD), lambda i, h: (i, k_blk0 + h)),
                  pl.BlockSpec((tt, pair), lambda i, h: (i, v_blk0 + h)),
                  pl.BlockSpec((tt, LANES), lambda i, h: (i, 0))],
        out_specs=[out_spec] * 5,
        out_shape=[jax.ShapeDtypeStruct((t, vd), F32)] + [jax.ShapeDtypeStruct((t, vd), BF16)] * 4,
        compiler_params=_params("parallel", "parallel"),
        name="gdn_prep",
    )(qkv, qkv, qkv, gb)


def _gdn_rec_kernel(u_ref, w_ref, qd_ref, ke_ref, at_ref, gb_ref, z_ref, nw_ref, o_ref, s_ref,
                    *, hb, cb):
    hg = pl.program_id(1)
    c = GDN_CHUNK

    @pl.when(pl.program_id(2) == 0)
    def _():
        s_ref[...] = jnp.zeros_like(s_ref)

    lane = lax.broadcasted_iota(jnp.int32, (1, LANES), 1)
    nw = nw_ref[...]

    def chunk_body(ci, carry):
        r0 = pl.multiple_of(ci * c, c)
        rows = pl.ds(r0, c)
        last = gb_ref[pl.ds(r0 + c - 1, 1), :]
        for hh in range(hb):
            head = hg * hb + hh
            cols = pl.ds(hh * GDN_HEAD, GDN_HEAD)
            g_last = _lane_pick(last, lane, GDN_V_HEADS + head)
            state = s_ref[hh]
            sb = state.astype(BF16)
            wq = jnp.concatenate([w_ref[rows, cols], qd_ref[rows, cols]], axis=0)
            a1 = jnp.dot(wq, sb, preferred_element_type=F32)
            v_new = u_ref[rows, cols] - a1[:c]
            vb = v_new.astype(BF16)
            o = a1[c:] + jnp.dot(at_ref[rows, pl.ds(hh * GDN_HEAD, c)], vb,
                                 preferred_element_type=F32)
            s_ref[hh] = state * jnp.exp(g_last) + lax.dot_general(
                ke_ref[rows, cols], vb, TN_DIMS, preferred_element_type=F32)
            on = o * lax.rsqrt(jnp.mean(o * o, axis=-1, keepdims=True) + GDN_NORM_EPS) * nw
            o_ref[rows, cols] = (on * _silu(z_ref[rows, cols])).astype(o_ref.dtype)
        return carry

    lax.fori_loop(0, cb, chunk_body, 0)


def _gdn_rec(u, w, qd, ke, at, gb, z, nw, batch, seq, *, hb=4, cb=4):
    t = u.shape[0]
    tt = cb * GDN_CHUNK
    nsteps = seq // tt
    blk = pl.BlockSpec((tt, hb * GDN_HEAD), lambda b, h, n: (b * nsteps + n, h))
    kern = functools.partial(_gdn_rec_kernel, hb=hb, cb=cb)
    return pl.pallas_call(
        kern,
        grid=(batch, GDN_V_HEADS // hb, nsteps),
        in_specs=[blk, blk, blk, blk, blk,
                  pl.BlockSpec((tt, LANES), lambda b, h, n: (b * nsteps + n, 0)),
                  blk,
                  pl.BlockSpec((1, GDN_HEAD), lambda b, h, n: (0, 0))],
        out_specs=blk,
        out_shape=jax.ShapeDtypeStruct((t, GDN_V_DIM), BF16),
        scratch_shapes=[pltpu.VMEM((hb, GDN_HEAD, GDN_HEAD), F32)],
        compiler_params=_params("parallel", "parallel", "arbitrary"),
        name="gdn_rec",
    )(u, w, qd, ke, at, gb, z, nw)


def _rope(x, cos_t, sin_lo, sin_hi):
    half = ROPE_DIM // 2
    return (x * cos_t + pltpu.roll(x, LANES - half, axis=1) * sin_lo
            + pltpu.roll(x, half, axis=1) * sin_hi)


def _rope_tables(seq):
    pos = jnp.arange(seq, dtype=F32)
    inv_freq = ROPE_THETA ** (-jnp.arange(0, ROPE_DIM, 2, dtype=F32) / ROPE_DIM)
    ang = pos[:, None] * inv_freq[None, :]
    cos, sin = jnp.cos(ang), jnp.sin(ang)
    half = ROPE_DIM // 2
    pad = LANES - ROPE_DIM
    cos_t = jnp.concatenate([cos, cos, jnp.ones((seq, pad), F32)], axis=1)
    sin_lo = jnp.concatenate([-sin, jnp.zeros((seq, LANES - half), F32)], axis=1)
    sin_hi = jnp.concatenate([jnp.zeros((seq, half), F32), sin, jnp.zeros((seq, pad), F32)], axis=1)
    return cos_t, sin_lo, sin_hi


def _kv_post_kernel(kv_ref, cos_ref, slo_ref, shi_ref, k_ref, v_ref, km_ref):
    cos_t, slo, shi = cos_ref[...], slo_ref[...], shi_ref[...]
    for hh in range(MOBA_KV_HEADS):
        cols = pl.ds(hh * MOBA_HEAD, MOBA_HEAD)
        kr = _rope(kv_ref[:, cols], cos_t, slo, shi)
        k_ref[:, cols] = kr.astype(BF16)
        km_ref[:, cols] = jnp.mean(kr, axis=0, keepdims=True)
    v_ref[...] = kv_ref[:, MOBA_KV_DIM:].astype(BF16)


def _kv_post(kv, tables, seq):
    t = kv.shape[0]
    nblk = seq // MOBA_BLOCK
    tab = pl.BlockSpec((MOBA_BLOCK, LANES), lambda i: (i % nblk, 0))
    return pl.pallas_call(
        _kv_post_kernel,
        grid=(t // MOBA_BLOCK,),
        in_specs=[pl.BlockSpec((MOBA_BLOCK, 2 * MOBA_KV_DIM), lambda i: (i, 0)), tab, tab, tab],
        out_specs=[pl.BlockSpec((MOBA_BLOCK, MOBA_KV_DIM), lambda i: (i, 0)),
                   pl.BlockSpec((MOBA_BLOCK, MOBA_KV_DIM), lambda i: (i, 0)),
                   pl.BlockSpec((None, 1, MOBA_KV_DIM), lambda i: (i, 0, 0))],
        out_shape=[jax.ShapeDtypeStruct((t, MOBA_KV_DIM), BF16),
                   jax.ShapeDtypeStruct((t, MOBA_KV_DIM), BF16),
                   jax.ShapeDtypeStruct((t // MOBA_BLOCK, 1, MOBA_KV_DIM), F32)],
        compiler_params=_params("parallel"),
        name="kv_post",
    )(kv, *tables)


def _moba_kernel(q_ref, cos_ref, slo_ref, shi_ref, k_ref, v_ref, km_ref, o_ref):
    cur = pl.program_id(2)
    blk = MOBA_BLOCK
    rows = MOBA_GROUP * blk
    scale = MOBA_HEAD ** -0.5
    cos_t, slo, shi = cos_ref[...], slo_ref[...], shi_ref[...]
    q = jnp.concatenate(
        [_rope(q_ref[:, g * MOBA_HEAD:(g + 1) * MOBA_HEAD], cos_t, slo, shi) * scale
         for g in range(MOBA_GROUP)], axis=0)
    qb = q.astype(BF16)

    nb = km_ref.shape[0]
    km = jnp.concatenate([km_ref[...], jnp.zeros((LANES - nb, MOBA_HEAD), F32)], axis=0)
    q_lo = (q - qb.astype(F32)).astype(BF16)
    kmb = km.astype(BF16)
    km_lo = (km - kmb.astype(F32)).astype(BF16)
    gate = lax.dot_general(jnp.concatenate([qb, qb, q_lo], axis=1),
                           jnp.concatenate([kmb, km_lo, kmb], axis=1),
                           NT_DIMS, preferred_element_type=F32)
    bid = lax.broadcasted_iota(jnp.int32, (rows, LANES), 1).astype(F32)
    neg_inf = -jnp.inf
    gm = jnp.where(bid < cur.astype(F32), gate, neg_inf)
    bias = jnp.full((rows, LANES), MASK_NEG, F32)
    for _ in range(MOBA_TOPK):
        m = jnp.max(gm, axis=1, keepdims=True)
        first = jnp.min(jnp.where(gm == m, bid, float(LANES)), axis=1, keepdims=True)
        pick = (bid == first) & (m > neg_inf)
        bias = jnp.where(pick, 0.0, bias)
        gm = jnp.where(pick, neg_inf, gm)
    qa = jnp.concatenate([qb, bias.astype(BF16)], axis=1)

    r_own = pl.ds(pl.multiple_of(cur * blk, blk), blk)
    s = lax.dot_general(qb, k_ref[r_own, :], NT_DIMS, preferred_element_type=F32)
    qpos = lax.broadcasted_iota(jnp.int32, (rows, blk), 0) % blk
    kpos = lax.broadcasted_iota(jnp.int32, (rows, blk), 1)
    s = jnp.where(kpos <= qpos, s, neg_inf)
    m0 = jnp.max(s, axis=1, keepdims=True)
    p = jnp.exp(s - m0)
    l0 = jnp.sum(p, axis=1, keepdims=True)
    acc0 = jnp.dot(p.astype(BF16), v_ref[r_own, :], preferred_element_type=F32)

    lane_k = lax.broadcasted_iota(jnp.int32, (blk, LANES), 1)

    def body(j, carry):
        m_i, l_i, acc = carry
        r_j = pl.ds(pl.multiple_of(j * blk, blk), blk)
        onehot = jnp.where(lane_k == j, 1.0, 0.0).astype(BF16)
        ka = jnp.concatenate([k_ref[r_j, :], onehot], axis=1)
        sj = lax.dot_general(qa, ka, NT_DIMS, preferred_element_type=F32)
        m_new = jnp.maximum(m_i, jnp.max(sj, axis=1, keepdims=True))
        alpha = jnp.exp(m_i - m_new)
        pj = jnp.exp(sj - m_new)
        l_new = alpha * l_i + jnp.sum(pj, axis=1, keepdims=True)
        acc_new = alpha * acc + jnp.dot(pj.astype(BF16), v_ref[r_j, :],
                                        preferred_element_type=F32)
        return m_new, l_new, acc_new

    _, l_f, acc_f = lax.fori_loop(0, cur, body, (m0, l0, acc0))
    out = acc_f / l_f
    for g in range(MOBA_GROUP):
        o_ref[:, g * MOBA_HEAD:(g + 1) * MOBA_HEAD] = out[g * blk:(g + 1) * blk].astype(o_ref.dtype)


def _moba(q, tables, kb, vb, km, batch, seq):
    t = q.shape[0]
    nblk = seq // MOBA_BLOCK
    gw = MOBA_GROUP * MOBA_HEAD
    tab = pl.BlockSpec((MOBA_BLOCK, LANES), lambda b, h, c: (c, 0))
    return pl.pallas_call(
        _moba_kernel,
        grid=(batch, MOBA_KV_HEADS, nblk),
        in_specs=[pl.BlockSpec((MOBA_BLOCK, gw), lambda b, h, c: (b * nblk + c, h)),
                  tab, tab, tab,
                  pl.BlockSpec((seq, MOBA_HEAD), lambda b, h, c: (b, h)),
                  pl.BlockSpec((seq, MOBA_HEAD), lambda b, h, c: (b, h)),
                  pl.BlockSpec((None, nblk, MOBA_HEAD), lambda b, h, c: (b, 0, h))],
        out_specs=pl.BlockSpec((MOBA_BLOCK, gw), lambda b, h, c: (b * nblk + c, h)),
        out_shape=jax.ShapeDtypeStruct((t, MOBA_Q_HEADS * MOBA_HEAD), BF16),
        compiler_params=_params("parallel", "parallel", "arbitrary"),
        name="moba_attn",
    )(q, *tables, kb, vb, km)


def _ffn_step(h, w_in, w_out, g, b):
    wg = w_in[:, 0, :].astype(BF16)
    wu = w_in[:, 1, :].astype(BF16)
    return _ffn_ln(h, wg, wu, w_out.astype(BF16), g[None, :], b[None, :])


def kernel(x, ln_g, ln_b, w_ffn_in, w_ffn_out, gdn_w_in, gdn_conv_w, gdn_a_log, gdn_dt_bias,
           gdn_norm_w, gdn_w_out, moba_w_kv, moba_w_q, moba_w_out):
    batch, seq, d = x.shape
    h = x.reshape(batch * seq, d)
    tables = _rope_tables(seq)

    h = _ffn_step(h, w_ffn_in[0, 0], w_ffn_out[0, 0], ln_g[0, 0], ln_b[0, 0])
    w_in = gdn_w_in[0]
    zoff = GDN_CONV_DIM + GDN_V_DIM
    w_ba = jnp.pad(w_in[:, zoff:], ((0, 0), (0, LANES - 2 * GDN_V_HEADS))).astype(BF16)
    pad_lo = jnp.zeros((GDN_V_HEADS,), F32)
    pad_hi = jnp.zeros((LANES - 2 * GDN_V_HEADS,), F32)
    alog = jnp.concatenate([pad_lo, gdn_a_log[0].astype(F32), pad_hi])[None, :]
    dtb = jnp.concatenate([pad_lo, gdn_dt_bias[0].astype(F32), pad_hi])[None, :]
    qkv_pre = _mm(h, w_in[:, :GDN_CONV_DIM].astype(BF16))
    z = _mm(h, w_in[:, GDN_CONV_DIM:zoff].astype(BF16))
    gb = _gdn_gates(h, w_ba, alog, dtb)
    qkv = _gdn_conv(qkv_pre, gdn_conv_w[0], seq)
    u, w, qd, ke, at = _gdn_prep(qkv, gb)
    o = _gdn_rec(u, w, qd, ke, at, gb, z, gdn_norm_w[0][None, :].astype(F32), batch, seq)
    h = _mm_res_ln(o, gdn_w_out[0].astype(BF16), h, ln_g[0, 1][None, :], ln_b[0, 1][None, :])
    h = _ffn_step(h, w_ffn_in[0, 1], w_ffn_out[0, 1], ln_g[0, 2], ln_b[0, 2])

    kv = _mm(h, moba_w_kv.astype(BF16))
    kb, vb, km = _kv_post(kv, tables, seq)
    km = km.reshape(batch, seq // MOBA_BLOCK, MOBA_KV_DIM)

    h = _ffn_step(h, w_ffn_in[1, 0], w_ffn_out[1, 0], ln_g[1, 0], ln_b[1, 0])
    q = _mm(h, moba_w_q[0].astype(BF16))
    o = _moba(q, tables, kb, vb, km, batch, seq)
    h = _mm_res_ln(o, moba_w_out[0].astype(BF16), h, ln_g[1, 1][None, :], ln_b[1, 1][None, :])
    h = _ffn_step(h, w_ffn_in[1, 1], w_ffn_out[1, 1], ln_g[1, 2], ln_b[1, 2])
    return h.reshape(batch, seq, d)
```

```python
import functools
import math

import jax
import jax.numpy as jnp
from jax import lax
from jax.experimental import pallas as pl
from jax.experimental.pallas import tpu as pltpu

F32 = jnp.float32
BF16 = jnp.bfloat16

DEPTH = 2
DEEPNORM_ALPHA = (2.0 * DEPTH) ** 0.25
LN_EPS = 1e-5
FFN_HALF = 0.5

GDN_QK_HEADS = 16
GDN_V_HEADS = 32
GDN_HEAD = 128
GDN_CONV = 4
GDN_CHUNK = 64
GDN_TILE = 256
GDN_BASE = 16
GDN_QK_DIM = GDN_QK_HEADS * GDN_HEAD
GDN_V_DIM = GDN_V_HEADS * GDN_HEAD
GDN_CONV_DIM = 2 * GDN_QK_DIM + GDN_V_DIM
GDN_NORM_EPS = 1e-6

MOBA_Q_HEADS = 16
MOBA_KV_HEADS = 4
MOBA_HEAD = 128
MOBA_GROUP = MOBA_Q_HEADS // MOBA_KV_HEADS
MOBA_KV_DIM = MOBA_KV_HEADS * MOBA_HEAD
MOBA_BLOCK = 256
MOBA_TOPK = 3
ROPE_THETA = 500000.0
ROPE_DIM = MOBA_HEAD // 4
MOBA_KV_GROUP = 4
MOBA_ONES_ROWS = 16

LANES = 128
VMEM_LIMIT = 56 * 1024 * 1024
MASK_NEG = -1e30

NT_DIMS = (((1,), (1,)), ((), ()))
TN_DIMS = (((0,), (0,)), ((), ()))


def _params(*sem):
    return pltpu.CompilerParams(dimension_semantics=sem, vmem_limit_bytes=VMEM_LIMIT)


def _layer_norm(y, g, b):
    mu = jnp.mean(y, axis=-1, keepdims=True)
    yc = y - mu
    var = jnp.mean(yc * yc, axis=-1, keepdims=True)
    return yc * lax.rsqrt(var + LN_EPS) * g + b


def _silu(x):
    return x * jax.nn.sigmoid(x)


def _ffn_ln_kernel(x_ref, wg_ref, wu_ref, wo_ref, g_ref, b_ref, o_ref, xb_ref, acc_ref):
    f = pl.program_id(1)

    @pl.when(f == 0)
    def _():
        xb_ref[...] = x_ref[...].astype(BF16)
        acc_ref[...] = jnp.zeros_like(acc_ref)

    xb = xb_ref[...]
    gate = jnp.dot(xb, wg_ref[...], preferred_element_type=F32)
    up = jnp.dot(xb, wu_ref[...], preferred_element_type=F32)
    mid = (_silu(gate) * up).astype(BF16)
    acc_ref[...] += jnp.dot(mid, wo_ref[...], preferred_element_type=F32)

    @pl.when(f == pl.num_programs(1) - 1)
    def _():
        y = DEEPNORM_ALPHA * x_ref[...] + FFN_HALF * acc_ref[...]
        o_ref[...] = _layer_norm(y, g_ref[...], b_ref[...])


def _ffn_ln(h, wg, wu, wo, g, b, *, tm=512, tf=512):
    t, d = h.shape
    f = wg.shape[1]
    return pl.pallas_call(
        _ffn_ln_kernel,
        grid=(t // tm, f // tf),
        in_specs=[
            pl.BlockSpec((tm, d), lambda i, j: (i, 0)),
            pl.BlockSpec((d, tf), lambda i, j: (0, j)),
            pl.BlockSpec((d, tf), lambda i, j: (0, j)),
            pl.BlockSpec((tf, d), lambda i, j: (j, 0)),
            pl.BlockSpec((1, d), lambda i, j: (0, 0)),
            pl.BlockSpec((1, d), lambda i, j: (0, 0)),
        ],
        out_specs=pl.BlockSpec((tm, d), lambda i, j: (i, 0)),
        out_shape=jax.ShapeDtypeStruct((t, d), F32),
        scratch_shapes=[pltpu.VMEM((tm, d), BF16), pltpu.VMEM((tm, d), F32)],
        compiler_params=_params("parallel", "arbitrary"),
        name="ffn_ln",
    )(h, wg, wu, wo, g, b)


def _mm_kernel(x_ref, w_ref, o_ref, xb_ref):
    @pl.when(pl.program_id(1) == 0)
    def _():
        xb_ref[...] = x_ref[...].astype(BF16)

    o_ref[...] = jnp.dot(xb_ref[...], w_ref[...], preferred_element_type=F32).astype(o_ref.dtype)


def _mm(x, w, *, tm=512, tn=1024, out_dtype=F32):
    t, k = x.shape
    n = w.shape[1]
    tn = min(tn, n)
    return pl.pallas_call(
        _mm_kernel,
        grid=(t // tm, n // tn),
        in_specs=[pl.BlockSpec((tm, k), lambda i, j: (i, 0)),
                  pl.BlockSpec((k, tn), lambda i, j: (0, j))],
        out_specs=pl.BlockSpec((tm, tn), lambda i, j: (i, j)),
        out_shape=jax.ShapeDtypeStruct((t, n), out_dtype),
        scratch_shapes=[pltpu.VMEM((tm, k), BF16)],
        compiler_params=_params("parallel", "arbitrary"),
        name="proj",
    )(x, w)


def _mm_res_ln_kernel(x_ref, w_ref, h_ref, g_ref, b_ref, o_ref, acc_ref):
    k = pl.program_id(1)

    @pl.when(k == 0)
    def _():
        acc_ref[...] = jnp.zeros_like(acc_ref)

    acc_ref[...] += jnp.dot(x_ref[...], w_ref[...], preferred_element_type=F32)

    @pl.when(k == pl.num_programs(1) - 1)
    def _():
        y = DEEPNORM_ALPHA * h_ref[...] + acc_ref[...]
        o_ref[...] = _layer_norm(y, g_ref[...], b_ref[...])


def _mm_res_ln(x, w, h, g, b, *, tm=512, tk=1024):
    t, kdim = x.shape
    d = w.shape[1]
    return pl.pallas_call(
        _mm_res_ln_kernel,
        grid=(t // tm, kdim // tk),
        in_specs=[
            pl.BlockSpec((tm, tk), lambda i, k: (i, k)),
            pl.BlockSpec((tk, d), lambda i, k: (k, 0)),
            pl.BlockSpec((tm, d), lambda i, k: (i, 0)),
            pl.BlockSpec((1, d), lambda i, k: (0, 0)),
            pl.BlockSpec((1, d), lambda i, k: (0, 0)),
        ],
        out_specs=pl.BlockSpec((tm, d), lambda i, k: (i, 0)),
        out_shape=jax.ShapeDtypeStruct((t, d), F32),
        scratch_shapes=[pltpu.VMEM((tm, d), F32)],
        compiler_params=_params("parallel", "arbitrary"),
        name="proj_res_ln",
    )(x, w, h, g, b)


def _gdn_gate_kernel(x_ref, w_ref, alog_ref, dtb_ref, o_ref):
    tm = x_ref.shape[0]
    logits = jnp.dot(x_ref[...].astype(BF16), w_ref[...], preferred_element_type=F32)
    lane = lax.broadcasted_iota(jnp.int32, (tm, LANES), 1)
    row = lax.broadcasted_iota(jnp.int32, (tm, LANES), 0) % GDN_CHUNK
    beta = jax.nn.sigmoid(logits)
    z = logits + dtb_ref[...]
    softplus = jnp.maximum(z, 0.0) + jnp.log1p(jnp.exp(-jnp.abs(z)))
    c = -jnp.exp(alog_ref[...]) * softplus
    shift = 1
    while shift < GDN_CHUNK:
        c = c + jnp.where(row >= shift, pltpu.roll(c, shift, axis=0), 0.0)
        shift *= 2
    o_ref[...] = jnp.where(lane < GDN_V_HEADS, beta, jnp.where(lane < 2 * GDN_V_HEADS, c, 0.0))


def _gdn_gates(h, w_ba, alog, dtb, *, tm=512):
    t, d = h.shape
    return pl.pallas_call(
        _gdn_gate_kernel,
        grid=(t // tm,),
        in_specs=[pl.BlockSpec((tm, d), lambda i: (i, 0)),
                  pl.BlockSpec((d, LANES), lambda i: (0, 0)),
                  pl.BlockSpec((1, LANES), lambda i: (0, 0)),
                  pl.BlockSpec((1, LANES), lambda i: (0, 0))],
        out_specs=pl.BlockSpec((tm, LANES), lambda i: (i, 0)),
        out_shape=jax.ShapeDtypeStruct((t, LANES), F32),
        compiler_params=_params("parallel"),
        name="gdn_gates",
    )(h, w_ba, alog, dtb)


def _gdn_conv_kernel(x_ref, halo_ref, w_ref, o_ref, buf_ref, *, tiles_per_seq, qk_blocks):
    i = pl.program_id(0)
    j = pl.program_id(1)
    tm, tc = x_ref.shape
    halo = jnp.where(i % tiles_per_seq == 0, 0.0, halo_ref[...])
    buf_ref[0:8, :] = halo
    buf_ref[8:, :] = x_ref[...]
    w = w_ref[...]
    y = x_ref[...] * w[GDN_CONV - 1:GDN_CONV, :]
    for back in range(1, GDN_CONV):
        y = y + buf_ref[pl.ds(8 - back, tm), :] * w[GDN_CONV - 1 - back:GDN_CONV - back, :]
    y = _silu(y)

    @pl.when(j < qk_blocks)
    def _():
        for hh in range(tc // LANES):
            seg = y[:, hh * LANES:(hh + 1) * LANES]
            ss = jnp.sum(seg * seg, axis=-1, keepdims=True)
            o_ref[:, hh * LANES:(hh + 1) * LANES] = seg * lax.rsqrt(ss + GDN_NORM_EPS)

    @pl.when(j >= qk_blocks)
    def _():
        o_ref[...] = y


def _gdn_conv(x, w, seq, *, tm=512, tc=512):
    t, c = x.shape
    kern = functools.partial(_gdn_conv_kernel, tiles_per_seq=seq // tm,
                             qk_blocks=2 * GDN_QK_DIM // tc)
    return pl.pallas_call(
        kern,
        grid=(t // tm, c // tc),
        in_specs=[pl.BlockSpec((tm, tc), lambda i, j: (i, j)),
                  pl.BlockSpec((8, tc), lambda i, j: (jnp.maximum(i * (tm // 8) - 1, 0), j)),
                  pl.BlockSpec((GDN_CONV, tc), lambda i, j: (0, j))],
        out_specs=pl.BlockSpec((tm, tc), lambda i, j: (i, j)),
        out_shape=jax.ShapeDtypeStruct((t, c), F32),
        scratch_shapes=[pltpu.VMEM((tm + 8, tc), F32)],
        compiler_params=_params("parallel", "parallel"),
        name="gdn_conv",
    )(x, x, w)


def _lane_pick(x, lane_idx, lane):
    return jnp.sum(jnp.where(lane_idx == lane, x, 0.0), axis=1, keepdims=True)


def _split3(x):
    hi = x.astype(BF16)
    r1 = x - hi.astype(F32)
    mid = r1.astype(BF16)
    lo = (r1 - mid.astype(F32)).astype(BF16)
    return hi.astype(F32), mid.astype(F32), lo.astype(F32)


def _gdn_prep_kernel(q_ref, k_ref, v_ref, gb_ref, u_ref, w_ref, qd_ref, ke_ref, at_ref, *, hps):
    hq0 = pl.program_id(1) * hps
    tt = GDN_TILE
    c = GDN_CHUNK
    scale = GDN_HEAD ** -0.5
    lane = lax.broadcasted_iota(jnp.int32, (tt, LANES), 1)
    ii = lax.broadcasted_iota(jnp.int32, (tt, tt), 0)
    jj = lax.broadcasted_iota(jnp.int32, (tt, tt), 1)

    def same(blk):
        sh = int(math.log2(blk))
        return lax.shift_right_logical(ii, sh) == lax.shift_right_logical(jj, sh)

    same_chunk = same(c)
    lower = same_chunk & (ii >= jj)
    strict = same_chunk & (ii > jj)
    eye = jnp.where(ii == jj, 1.0, 0.0)
    is_chunk_end = jj == (ii | (c - 1))
    gbc = gb_ref[...]
    same_base = same(GDN_BASE)
    heads = [(hl, hv) for hl in range(hps) for hv in range(2)]
    kc, qc, kk, qk = {}, {}, {}, {}
    for hl in range(hps):
        kq_cols = pl.ds(hl * GDN_HEAD, GDN_HEAD)
        kc[hl] = k_ref[:, kq_cols]
        qc[hl] = q_ref[:, kq_cols] * scale
        kbf = kc[hl].astype(BF16)
        kk[hl] = lax.dot_general(kbf, kbf, NT_DIMS, preferred_element_type=F32)
        qk[hl] = lax.dot_general(qc[hl].astype(BF16), kbf, NT_DIMS, preferred_element_type=F32)
    beta, gc, gdiff, lfull, p, tinv = {}, {}, {}, {}, {}, {}
    for hd in heads:
        hl, hv = hd
        head = 2 * (hq0 + hl) + hv
        cols = pl.ds((2 * hl + hv) * GDN_HEAD, GDN_HEAD)
        beta[hd] = _lane_pick(gbc, lane, head)
        gc[hd] = _lane_pick(gbc, lane, GDN_V_HEADS + head)
        hi, mid, lo = _split3(gc[hd])
        a_mat = jnp.where(lane == 0, hi, jnp.where(lane == 1, mid, jnp.where(
            lane == 2, lo, jnp.where(lane < 6, 1.0, 0.0)))).astype(BF16)
        b_mat = jnp.where(lane < 3, 1.0, jnp.where(lane == 3, -hi, jnp.where(
            lane == 4, -mid, jnp.where(lane == 5, -lo, 0.0)))).astype(BF16)
        gdiff[hd] = lax.dot_general(a_mat, b_mat, NT_DIMS, preferred_element_type=F32)
        decay = jnp.where(lower, jnp.exp(jnp.where(lower, gdiff[hd], 0.0)), 0.0)
        lfull[hd] = jnp.where(strict, kk[hl] * beta[hd] * decay, 0.0)
        attn = qk[hl] * decay
        for ch in range(tt // c):
            lt = (ch * c) // LANES
            at_ref[ch * c:(ch + 1) * c, cols] = attn[ch * c:(ch + 1) * c,
                                                     lt * LANES:(lt + 1) * LANES].astype(BF16)
        eg = jnp.exp(gc[hd])
        to_end = jnp.exp(-jnp.sum(jnp.where(is_chunk_end, gdiff[hd], 0.0), axis=1, keepdims=True))
        qd_ref[:, cols] = (qc[hl] * eg).astype(BF16)
        ke_ref[:, cols] = (kc[hl] * to_end).astype(BF16)
        p[hd] = -jnp.where(same_base, lfull[hd], 0.0)
        tinv[hd] = eye + p[hd]
    for _ in range(int(math.log2(GDN_BASE)) - 1):
        for hd in heads:
            pb = p[hd].astype(BF16)
            p[hd] = jnp.dot(pb, pb, preferred_element_type=F32)
        for hd in heads:
            tinv[hd] = tinv[hd] + jnp.dot(tinv[hd].astype(BF16), p[hd].astype(BF16),
                                          preferred_element_type=F32)
    width = GDN_BASE
    while width < c:
        off_mask = same(2 * width) & jnp.logical_not(same(width))
        inner, tb = {}, {}
        for hd in heads:
            tb[hd] = tinv[hd].astype(BF16)
            inner[hd] = jnp.dot(jnp.where(off_mask, lfull[hd], 0.0).astype(BF16), tb[hd],
                                preferred_element_type=F32)
        for hd in heads:
            tinv[hd] = tinv[hd] - jnp.dot(tb[hd], inner[hd].astype(BF16),
                                          preferred_element_type=F32)
        width *= 2
    for hd in heads:
        hl, hv = hd
        cols = pl.ds((2 * hl + hv) * GDN_HEAD, GDN_HEAD)
        rhs = jnp.concatenate([v_ref[:, cols] * beta[hd],
                               kc[hl] * beta[hd] * jnp.exp(gc[hd])], axis=1)
        x = rhs + jnp.dot((tinv[hd] - eye).astype(BF16), rhs.astype(BF16),
                          preferred_element_type=F32)
        u_ref[:, cols] = x[:, :GDN_HEAD]
        w_ref[:, cols] = x[:, GDN_HEAD:].astype(BF16)


def _gdn_prep(qkv, gb, *, hps=2):
    t = qkv.shape[0]
    tt = GDN_TILE
    vd = GDN_V_DIM
    qw = hps * GDN_HEAD
    vw = 2 * qw
    k_blk0 = GDN_QK_DIM // qw
    v_blk0 = 2 * GDN_QK_DIM // vw
    out_spec = pl.BlockSpec((tt, vw), lambda i, h: (i, h))
    kern = functools.partial(_gdn_prep_kernel, hps=hps)
    return pl.pallas_call(
        kern,
        grid=(t // tt, GDN_QK_HEADS // hps),
        in_specs=[pl.BlockSpec((tt, qw), lambda i, h: (i, h)),
                  pl.BlockSpec((tt, qw), lambda i, h: (i, k_blk0 + h)),
                  pl.BlockSpec((tt, vw), lambda i, h: (i, v_blk0 + h)),
                  pl.BlockSpec((tt, LANES), lambda i, h: (i, 0))],
        out_specs=[out_spec] * 5,
        out_shape=[jax.ShapeDtypeStruct((t, vd), F32)] + [jax.ShapeDtypeStruct((t, vd), BF16)] * 4,
        compiler_params=_params("parallel", "parallel"),
        name="gdn_prep",
    )(qkv, qkv, qkv, gb)


def _gdn_rec_kernel(u_ref, w_ref, qd_ref, ke_ref, at_ref, gb_ref, z_ref, nw_ref, o_ref, s_ref,
                    vp_ref, *, hb, cb):
    hg = pl.program_id(1)
    c = GDN_CHUNK

    @pl.when(pl.program_id(2) == 0)
    def _():
        s_ref[...] = jnp.zeros_like(s_ref)
        vp_ref[...] = jnp.zeros_like(vp_ref)

    lane = lax.broadcasted_iota(jnp.int32, (1, LANES), 1)
    nw = nw_ref[...]

    def pair_body(pi, carry):
        for half in range(2):
            r0 = pl.multiple_of(pi * (2 * c) + half * c, c)
            rows = pl.ds(r0, c)
            last = gb_ref[pl.ds(r0 + c - 1, 1), :]
            cols = [pl.ds(hh * GDN_HEAD, GDN_HEAD) for hh in range(hb)]
            state = [s_ref[hh] for hh in range(hb)]
            a1 = [jnp.dot(jnp.concatenate([w_ref[rows, cols[hh]], qd_ref[rows, cols[hh]]], axis=0),
                          state[hh].astype(BF16), preferred_element_type=F32) for hh in range(hb)]
            vb = [(u_ref[rows, cols[hh]] - a1[hh][:c]).astype(BF16) for hh in range(hb)]
            for hh in range(hb):
                vp_ref[hh, half * c:(half + 1) * c, :] = vb[hh]
            o = [a1[hh][c:] + jnp.dot(at_ref[rows, cols[hh]], vp_ref[hh],
                                      preferred_element_type=F32) for hh in range(hb)]
            for hh in range(hb):
                g_last = _lane_pick(last, lane, GDN_V_HEADS + hg * hb + hh)
                s_ref[hh] = state[hh] * jnp.exp(g_last) + lax.dot_general(
                    ke_ref[rows, cols[hh]], vb[hh], TN_DIMS, preferred_element_type=F32)
            for hh in range(hb):
                on = o[hh] * lax.rsqrt(jnp.mean(o[hh] * o[hh], axis=-1, keepdims=True)
                                       + GDN_NORM_EPS) * nw
                o_ref[rows, cols[hh]] = (on * _silu(z_ref[rows, cols[hh]])).astype(o_ref.dtype)
        return carry

    lax.fori_loop(0, cb // 2, pair_body, 0)


def _gdn_rec(u, w, qd, ke, at, gb, z, nw, batch, seq, *, hb=8, cb=4):
    t = u.shape[0]
    tt = cb * GDN_CHUNK
    nsteps = seq // tt
    blk = pl.BlockSpec((tt, hb * GDN_HEAD), lambda b, h, n: (b * nsteps + n, h))
    kern = functools.partial(_gdn_rec_kernel, hb=hb, cb=cb)
    return pl.pallas_call(
        kern,
        grid=(batch, GDN_V_HEADS // hb, nsteps),
        in_specs=[blk, blk, blk, blk, blk,
                  pl.BlockSpec((tt, LANES), lambda b, h, n: (b * nsteps + n, 0)),
                  blk,
                  pl.BlockSpec((1, GDN_HEAD), lambda b, h, n: (0, 0))],
        out_specs=blk,
        out_shape=jax.ShapeDtypeStruct((t, GDN_V_DIM), BF16),
        scratch_shapes=[pltpu.VMEM((hb, GDN_HEAD, GDN_HEAD), F32),
                        pltpu.VMEM((hb, 2 * GDN_CHUNK, GDN_HEAD), BF16)],
        compiler_params=_params("parallel", "parallel", "arbitrary"),
        name="gdn_rec",
    )(u, w, qd, ke, at, gb, z, nw)


def _rope(x, cos_t, sin_lo, sin_hi):
    half = ROPE_DIM // 2
    return (x * cos_t + pltpu.roll(x, LANES - half, axis=1) * sin_lo
            + pltpu.roll(x, half, axis=1) * sin_hi)


def _rope_tables(seq):
    pos = jnp.arange(seq, dtype=F32)
    inv_freq = ROPE_THETA ** (-jnp.arange(0, ROPE_DIM, 2, dtype=F32) / ROPE_DIM)
    ang = pos[:, None] * inv_freq[None, :]
    cos, sin = jnp.cos(ang), jnp.sin(ang)
    half = ROPE_DIM // 2
    pad = LANES - ROPE_DIM
    cos_t = jnp.concatenate([cos, cos, jnp.ones((seq, pad), F32)], axis=1)
    sin_lo = jnp.concatenate([-sin, jnp.zeros((seq, LANES - half), F32)], axis=1)
    sin_hi = jnp.concatenate([jnp.zeros((seq, half), F32), sin, jnp.zeros((seq, pad), F32)], axis=1)
    return cos_t, sin_lo, sin_hi


def _kv_post_kernel(kv_ref, cos_ref, slo_ref, shi_ref, k_ref, vt_ref, km_ref):
    cos_t, slo, shi = cos_ref[...], slo_ref[...], shi_ref[...]
    for hh in range(MOBA_KV_HEADS):
        cols = pl.ds(hh * MOBA_HEAD, MOBA_HEAD)
        kr = _rope(kv_ref[:, cols], cos_t, slo, shi)
        k_ref[:, cols] = kr.astype(BF16)
        km_ref[:, cols] = jnp.mean(kr, axis=0, keepdims=True)
        v = kv_ref[:, pl.ds(MOBA_KV_DIM + hh * MOBA_HEAD, MOBA_HEAD)]
        vt_ref[hh, 0:MOBA_HEAD, :] = v.T.astype(BF16)
        vt_ref[hh, MOBA_HEAD:, :] = jnp.ones((MOBA_ONES_ROWS, MOBA_BLOCK), BF16)


def _kv_post(kv, tables, batch, seq):
    t = kv.shape[0]
    nblk = seq // MOBA_BLOCK
    vrows = MOBA_HEAD + MOBA_ONES_ROWS
    tab = pl.BlockSpec((MOBA_BLOCK, LANES), lambda i: (i % nblk, 0))
    return pl.pallas_call(
        _kv_post_kernel,
        grid=(t // MOBA_BLOCK,),
        in_specs=[pl.BlockSpec((MOBA_BLOCK, 2 * MOBA_KV_DIM), lambda i: (i, 0)), tab, tab, tab],
        out_specs=[pl.BlockSpec((MOBA_BLOCK, MOBA_KV_DIM), lambda i: (i, 0)),
                   pl.BlockSpec((None, MOBA_KV_HEADS, vrows, MOBA_BLOCK),
                                lambda i: (i // nblk, 0, 0, i % nblk)),
                   pl.BlockSpec((None, 1, MOBA_KV_DIM), lambda i: (i, 0, 0))],
        out_shape=[jax.ShapeDtypeStruct((t, MOBA_KV_DIM), BF16),
                   jax.ShapeDtypeStruct((batch, MOBA_KV_HEADS, vrows, seq), BF16),
                   jax.ShapeDtypeStruct((t // MOBA_BLOCK, 1, MOBA_KV_DIM), F32)],
        compiler_params=_params("parallel"),
        name="kv_post",
    )(kv, *tables)


def _moba_kernel(q_ref, cos_ref, slo_ref, shi_ref, k_ref, vt_ref, km_ref, o_ref,
                 qa_ref, m_ref, acc_ref, sa_ref, sb_ref):
    cur = pl.program_id(2)
    blk = MOBA_BLOCK
    nq = MOBA_GROUP * blk
    nblk = k_ref.shape[0] // blk
    kvt = MOBA_KV_GROUP * blk
    scale = MOBA_HEAD ** -0.5 * math.log2(math.e)
    neg_inf = -jnp.inf
    cols = [pl.ds(g * blk, blk) for g in range(MOBA_GROUP)]
    cos_t, slo, shi = cos_ref[...], slo_ref[...], shi_ref[...]
    q_t = jnp.concatenate(
        [(_rope(q_ref[:, g * MOBA_HEAD:(g + 1) * MOBA_HEAD], cos_t, slo, shi) * scale).T
         for g in range(MOBA_GROUP)], axis=1)
    qb = q_t.astype(BF16)
    qa_ref[0:MOBA_HEAD, :] = qb

    nb = km_ref.shape[0]
    km = jnp.concatenate([km_ref[...], jnp.zeros((LANES - nb, MOBA_HEAD), F32)], axis=0)
    q_lo = (q_t - qb.astype(F32)).astype(BF16)
    kmb = km.astype(BF16)
    km_lo = (km - kmb.astype(F32)).astype(BF16)
    gate = jnp.dot(jnp.concatenate([kmb, km_lo, kmb], axis=1),
                   jnp.concatenate([qb, qb, q_lo], axis=0),
                   preferred_element_type=F32)
    bid = lax.broadcasted_iota(jnp.int32, (LANES, nq), 0).astype(F32)
    gm = jnp.where(bid < cur.astype(F32), gate, neg_inf)
    bias = jnp.full((LANES, nq), MASK_NEG, F32)
    for _ in range(MOBA_TOPK):
        m = jnp.max(gm, axis=0, keepdims=True)
        first = jnp.min(jnp.where(gm == m, bid, float(LANES)), axis=0, keepdims=True)
        pick = (bid == first) & (m > neg_inf)
        bias = jnp.where(pick, 0.0, bias)
        gm = jnp.where(pick, neg_inf, gm)
    qa_ref[MOBA_HEAD:, :] = bias.astype(BF16)

    lane_k = lax.broadcasted_iota(jnp.int32, (kvt, LANES), 1)
    blk_of_row = lax.shift_right_logical(lax.broadcasted_iota(jnp.int32, (kvt, LANES), 0),
                                         int(math.log2(blk)))

    def group_rows(gi):
        j0 = jnp.minimum(gi * MOBA_KV_GROUP, nblk - MOBA_KV_GROUP)
        return pl.ds(pl.multiple_of(j0 * blk, kvt), kvt)

    def scores(gi, s_ref):
        onehot = jnp.where(lane_k == gi * MOBA_KV_GROUP + blk_of_row, 1.0, 0.0).astype(BF16)
        ka = jnp.concatenate([k_ref[group_rows(gi), :], onehot], axis=1)
        for g in range(MOBA_GROUP):
            s_ref[:, cols[g]] = jnp.dot(ka, qa_ref[:, cols[g]], preferred_element_type=F32)

    def accumulate(gi, s_ref):
        vt = vt_ref[:, group_rows(gi)]
        sj = [s_ref[:, cols[g]] for g in range(MOBA_GROUP)]
        m_old = [m_ref[:, cols[g]] for g in range(MOBA_GROUP)]
        m_new = [jnp.maximum(m_old[g], jnp.max(sj[g], axis=0, keepdims=True))
                 for g in range(MOBA_GROUP)]
        pj = [jnp.exp2(sj[g] - m_new[g]).astype(BF16) for g in range(MOBA_GROUP)]
        pv = [jnp.dot(vt, pj[g], preferred_element_type=F32) for g in range(MOBA_GROUP)]
        for g in range(MOBA_GROUP):
            m_ref[:, cols[g]] = m_new[g]
            acc_ref[:, cols[g]] = jnp.exp2(m_old[g] - m_new[g]) * acc_ref[:, cols[g]] + pv[g]

    scores(0, sa_ref)

    c_own = pl.ds(pl.multiple_of(cur * blk, blk), blk)
    s = jnp.dot(k_ref[c_own, :], qb, preferred_element_type=F32)
    kpos = lax.broadcasted_iota(jnp.int32, (blk, nq), 0)
    qpos = lax.broadcasted_iota(jnp.int32, (blk, nq), 1) % blk
    s = jnp.where(kpos <= qpos, s, neg_inf)
    m0 = jnp.max(s, axis=0, keepdims=True)
    m_ref[...] = m0
    acc_ref[...] = jnp.dot(vt_ref[:, c_own], jnp.exp2(s - m0).astype(BF16),
                           preferred_element_type=F32)

    def pair_body(pi, carry):
        scores(2 * pi + 1, sb_ref)
        accumulate(2 * pi, sa_ref)
        scores(2 * pi + 2, sa_ref)
        accumulate(2 * pi + 1, sb_ref)
        return carry

    lax.fori_loop(0, (cur + 2 * MOBA_KV_GROUP - 1) // (2 * MOBA_KV_GROUP), pair_body, 0)
    out_t = acc_ref[0:MOBA_HEAD, :] / acc_ref[MOBA_HEAD:MOBA_HEAD + 1, :]
    for g in range(MOBA_GROUP):
        o_ref[:, g * MOBA_HEAD:(g + 1) * MOBA_HEAD] = out_t[:, g * blk:(g + 1) * blk].T.astype(
            o_ref.dtype)


def _moba(q, tables, kb, vt, km, batch, seq):
    t = q.shape[0]
    nblk = seq // MOBA_BLOCK
    gw = MOBA_GROUP * MOBA_HEAD
    nq = MOBA_GROUP * MOBA_BLOCK
    vrows = MOBA_HEAD + MOBA_ONES_ROWS
    tab = pl.BlockSpec((MOBA_BLOCK, LANES), lambda b, h, c: (c, 0))
    return pl.pallas_call(
        _moba_kernel,
        grid=(batch, MOBA_KV_HEADS, nblk),
        in_specs=[pl.BlockSpec((MOBA_BLOCK, gw), lambda b, h, c: (b * nblk + c, h)),
                  tab, tab, tab,
                  pl.BlockSpec((seq, MOBA_HEAD), lambda b, h, c: (b, h)),
                  pl.BlockSpec((None, None, vrows, seq), lambda b, h, c: (b, h, 0, 0)),
                  pl.BlockSpec((None, nblk, MOBA_HEAD), lambda b, h, c: (b, 0, h))],
        out_specs=pl.BlockSpec((MOBA_BLOCK, gw), lambda b, h, c: (b * nblk + c, h)),
        out_shape=jax.ShapeDtypeStruct((t, MOBA_Q_HEADS * MOBA_HEAD), BF16),
        scratch_shapes=[pltpu.VMEM((2 * MOBA_HEAD, nq), BF16),
                        pltpu.VMEM((1, nq), F32),
                        pltpu.VMEM((vrows, nq), F32),
                        pltpu.VMEM((MOBA_KV_GROUP * MOBA_BLOCK, nq), F32),
                        pltpu.VMEM((MOBA_KV_GROUP * MOBA_BLOCK, nq), F32)],
        compiler_params=_params("parallel", "parallel", "arbitrary"),
        name="moba_attn",
    )(q, *tables, kb, vt, km)


def _ffn_step(h, w_in, w_out, g, b):
    wg = w_in[:, 0, :].astype(BF16)
    wu = w_in[:, 1, :].astype(BF16)
    return _ffn_ln(h, wg, wu, w_out.astype(BF16), g[None, :], b[None, :])


def kernel(x, ln_g, ln_b, w_ffn_in, w_ffn_out, gdn_w_in, gdn_conv_w, gdn_a_log, gdn_dt_bias,
           gdn_norm_w, gdn_w_out, moba_w_kv, moba_w_q, moba_w_out):
    batch, seq, d = x.shape
    h = x.reshape(batch * seq, d)
    tables = _rope_tables(seq)

    h = _ffn_step(h, w_ffn_in[0, 0], w_ffn_out[0, 0], ln_g[0, 0], ln_b[0, 0])
    w_in = gdn_w_in[0]
    zoff = GDN_CONV_DIM + GDN_V_DIM
    w_ba = jnp.pad(w_in[:, zoff:], ((0, 0), (0, LANES - 2 * GDN_V_HEADS))).astype(BF16)
    pad_lo = jnp.zeros((GDN_V_HEADS,), F32)
    pad_hi = jnp.zeros((LANES - 2 * GDN_V_HEADS,), F32)
    alog = jnp.concatenate([pad_lo, gdn_a_log[0].astype(F32), pad_hi])[None, :]
    dtb = jnp.concatenate([pad_lo, gdn_dt_bias[0].astype(F32), pad_hi])[None, :]
    qkv_pre = _mm(h, w_in[:, :GDN_CONV_DIM].astype(BF16))
    z = _mm(h, w_in[:, GDN_CONV_DIM:zoff].astype(BF16))
    gb = _gdn_gates(h, w_ba, alog, dtb)
    qkv = _gdn_conv(qkv_pre, gdn_conv_w[0], seq)
    u, w, qd, ke, at = _gdn_prep(qkv, gb)
    o = _gdn_rec(u, w, qd, ke, at, gb, z, gdn_norm_w[0][None, :].astype(F32), batch, seq)
    h = _mm_res_ln(o, gdn_w_out[0].astype(BF16), h, ln_g[0, 1][None, :], ln_b[0, 1][None, :])
    h = _ffn_step(h, w_ffn_in[0, 1], w_ffn_out[0, 1], ln_g[0, 2], ln_b[0, 2])

    kv = _mm(h, moba_w_kv.astype(BF16))
    kb, vt, km = _kv_post(kv, tables, batch, seq)
    km = km.reshape(batch, seq // MOBA_BLOCK, MOBA_KV_DIM)

    h = _ffn_step(h, w_ffn_in[1, 0], w_ffn_out[1, 0], ln_g[1, 0], ln_b[1, 0])
    q = _mm(h, moba_w_q[0].astype(BF16))
    o = _moba(q, tables, kb, vt, km, batch, seq)
    h = _mm_res_ln(o, moba_w_out[0].astype(BF16), h, ln_g[1, 1][None, :], ln_b[1, 1][None, :])
    h = _ffn_step(h, w_ffn_in[1, 1], w_ffn_out[1, 1], ln_g[1, 2], ln_b[1, 2])
    return h.reshape(batch, seq, d)
```

```python
import functools
import math

import jax
import jax.numpy as jnp
from jax import lax
from jax.experimental import pallas as pl
from jax.experimental.pallas import tpu as pltpu

F32 = jnp.float32
BF16 = jnp.bfloat16

DEPTH = 2
DEEPNORM_ALPHA = (2.0 * DEPTH) ** 0.25
LN_EPS = 1e-5
FFN_HALF = 0.5

GDN_QK_HEADS = 16
GDN_V_HEADS = 32
GDN_HEAD = 128
GDN_CONV = 4
GDN_CHUNK = 64
GDN_TILE = 256
GDN_BASE = 16
GDN_QK_DIM = GDN_QK_HEADS * GDN_HEAD
GDN_V_DIM = GDN_V_HEADS * GDN_HEAD
GDN_CONV_DIM = 2 * GDN_QK_DIM + GDN_V_DIM
GDN_NORM_EPS = 1e-6

MOBA_Q_HEADS = 16
MOBA_KV_HEADS = 4
MOBA_HEAD = 128
MOBA_GROUP = MOBA_Q_HEADS // MOBA_KV_HEADS
MOBA_KV_DIM = MOBA_KV_HEADS * MOBA_HEAD
MOBA_BLOCK = 256
MOBA_TOPK = 3
ROPE_THETA = 500000.0
ROPE_DIM = MOBA_HEAD // 4
MOBA_KV_GROUP = 4
MOBA_ONES_ROWS = 16

LANES = 128
VMEM_LIMIT = 56 * 1024 * 1024
MASK_NEG = -1e30

NT_DIMS = (((1,), (1,)), ((), ()))
TN_DIMS = (((0,), (0,)), ((), ()))


def _params(*sem):
    return pltpu.CompilerParams(dimension_semantics=sem, vmem_limit_bytes=VMEM_LIMIT)


def _layer_norm(y, g, b):
    mu = jnp.mean(y, axis=-1, keepdims=True)
    yc = y - mu
    var = jnp.mean(yc * yc, axis=-1, keepdims=True)
    return yc * lax.rsqrt(var + LN_EPS) * g + b


def _silu(x):
    return x * jax.nn.sigmoid(x)


def _ffn_ln_kernel(x_ref, wg_ref, wu_ref, wo_ref, g_ref, b_ref, o_ref, xb_ref, acc_ref):
    f = pl.program_id(1)

    @pl.when(f == 0)
    def _():
        xb_ref[...] = x_ref[...].astype(BF16)
        acc_ref[...] = jnp.zeros_like(acc_ref)

    xb = xb_ref[...]
    gate = jnp.dot(xb, wg_ref[...], preferred_element_type=F32)
    up = jnp.dot(xb, wu_ref[...], preferred_element_type=F32)
    mid = (_silu(gate) * up).astype(BF16)
    acc_ref[...] += jnp.dot(mid, wo_ref[...], preferred_element_type=F32)

    @pl.when(f == pl.num_programs(1) - 1)
    def _():
        y = DEEPNORM_ALPHA * x_ref[...] + FFN_HALF * acc_ref[...]
        o_ref[...] = _layer_norm(y, g_ref[...], b_ref[...])


def _ffn_ln(h, w_in, wo, g, b, *, tm=512, tf=512):
    t, d = h.shape
    f = wo.shape[0]
    nf = f // tf
    return pl.pallas_call(
        _ffn_ln_kernel,
        grid=(t // tm, nf),
        in_specs=[
            pl.BlockSpec((tm, d), lambda i, j: (i, 0)),
            pl.BlockSpec((d, tf), lambda i, j: (0, j)),
            pl.BlockSpec((d, tf), lambda i, j: (0, nf + j)),
            pl.BlockSpec((tf, d), lambda i, j: (j, 0)),
            pl.BlockSpec((1, d), lambda i, j: (0, 0)),
            pl.BlockSpec((1, d), lambda i, j: (0, 0)),
        ],
        out_specs=pl.BlockSpec((tm, d), lambda i, j: (i, 0)),
        out_shape=jax.ShapeDtypeStruct((t, d), F32),
        scratch_shapes=[pltpu.VMEM((tm, d), BF16), pltpu.VMEM((tm, d), F32)],
        compiler_params=_params("parallel", "arbitrary"),
        name="ffn_ln",
    )(h, w_in, w_in, wo, g, b)


def _mm_kernel(x_ref, w_ref, o_ref, xb_ref):
    @pl.when(pl.program_id(1) == 0)
    def _():
        xb_ref[...] = x_ref[...].astype(BF16)

    o_ref[...] = jnp.dot(xb_ref[...], w_ref[...], preferred_element_type=F32).astype(o_ref.dtype)


def _mm(x, w, *, col0=0, n=None, tm=512, tn=1024, out_dtype=F32):
    t, k = x.shape
    n = w.shape[1] if n is None else n
    tn = min(tn, n)
    blk0 = col0 // tn
    return pl.pallas_call(
        _mm_kernel,
        grid=(t // tm, n // tn),
        in_specs=[pl.BlockSpec((tm, k), lambda i, j: (i, 0)),
                  pl.BlockSpec((k, tn), lambda i, j: (0, blk0 + j))],
        out_specs=pl.BlockSpec((tm, tn), lambda i, j: (i, j)),
        out_shape=jax.ShapeDtypeStruct((t, n), out_dtype),
        scratch_shapes=[pltpu.VMEM((tm, k), BF16)],
        compiler_params=_params("parallel", "arbitrary"),
        name="proj",
    )(x, w)


def _mm_res_ln_kernel(x_ref, w_ref, h_ref, g_ref, b_ref, o_ref, acc_ref):
    k = pl.program_id(1)

    @pl.when(k == 0)
    def _():
        acc_ref[...] = jnp.zeros_like(acc_ref)

    acc_ref[...] += jnp.dot(x_ref[...], w_ref[...], preferred_element_type=F32)

    @pl.when(k == pl.num_programs(1) - 1)
    def _():
        y = DEEPNORM_ALPHA * h_ref[...] + acc_ref[...]
        o_ref[...] = _layer_norm(y, g_ref[...], b_ref[...])


def _mm_res_ln(x, w, h, g, b, *, tm=512, tk=1024):
    t, kdim = x.shape
    d = w.shape[1]
    return pl.pallas_call(
        _mm_res_ln_kernel,
        grid=(t // tm, kdim // tk),
        in_specs=[
            pl.BlockSpec((tm, tk), lambda i, k: (i, k)),
            pl.BlockSpec((tk, d), lambda i, k: (k, 0)),
            pl.BlockSpec((tm, d), lambda i, k: (i, 0)),
            pl.BlockSpec((1, d), lambda i, k: (0, 0)),
            pl.BlockSpec((1, d), lambda i, k: (0, 0)),
        ],
        out_specs=pl.BlockSpec((tm, d), lambda i, k: (i, 0)),
        out_shape=jax.ShapeDtypeStruct((t, d), F32),
        scratch_shapes=[pltpu.VMEM((tm, d), F32)],
        compiler_params=_params("parallel", "arbitrary"),
        name="proj_res_ln",
    )(x, w, h, g, b)


def _gdn_gate_kernel(x_ref, w_ref, alog_ref, dtb_ref, o_ref):
    tm = x_ref.shape[0]
    logits = jnp.dot(x_ref[...].astype(BF16), w_ref[...], preferred_element_type=F32)
    lane = lax.broadcasted_iota(jnp.int32, (tm, LANES), 1)
    row = lax.broadcasted_iota(jnp.int32, (tm, LANES), 0) % GDN_CHUNK
    beta = jax.nn.sigmoid(logits)
    z = logits + dtb_ref[...]
    softplus = jnp.maximum(z, 0.0) + jnp.log1p(jnp.exp(-jnp.abs(z)))
    c = -jnp.exp(alog_ref[...]) * softplus
    shift = 1
    while shift < GDN_CHUNK:
        c = c + jnp.where(row >= shift, pltpu.roll(c, shift, axis=0), 0.0)
        shift *= 2
    o_ref[...] = jnp.where(lane < GDN_V_HEADS, beta, jnp.where(lane < 2 * GDN_V_HEADS, c, 0.0))


def _gdn_gates(h, w_ba, alog, dtb, *, tm=512):
    t, d = h.shape
    return pl.pallas_call(
        _gdn_gate_kernel,
        grid=(t // tm,),
        in_specs=[pl.BlockSpec((tm, d), lambda i: (i, 0)),
                  pl.BlockSpec((d, LANES), lambda i: (0, 0)),
                  pl.BlockSpec((1, LANES), lambda i: (0, 0)),
                  pl.BlockSpec((1, LANES), lambda i: (0, 0))],
        out_specs=pl.BlockSpec((tm, LANES), lambda i: (i, 0)),
        out_shape=jax.ShapeDtypeStruct((t, LANES), F32),
        compiler_params=_params("parallel"),
        name="gdn_gates",
    )(h, w_ba, alog, dtb)


def _gdn_qkv_kernel(x_ref, w_ref, cw_ref, o_ref, xb_ref, pre_ref, carry_ref,
                    *, ncb, tiles_per_seq, qk_blocks):
    s = pl.program_id(0)
    tm = x_ref.shape[0]
    tail = carry_ref.shape[1]
    tn = w_ref.shape[1]

    @pl.when(s == 0)
    def _():
        pre_ref[...] = jnp.zeros_like(pre_ref)
        carry_ref[...] = jnp.zeros_like(carry_ref)

    @pl.when(s % ncb == 0)
    def _():
        xb_ref[...] = x_ref[...].astype(BF16)

    sp = jnp.maximum(s - 1, 0)
    ip, jp = sp // ncb, sp % ncb
    prev = pre_ref.at[(s + 1) % 2]
    prev[0:tail, :] = jnp.where(ip % tiles_per_seq == 0, 0.0, carry_ref[jp])
    cur_rows = prev[tail:, :]
    carry_ref[jp] = cur_rows[tm - tail:, :]
    cw = cw_ref[...]
    y = cur_rows * cw[GDN_CONV - 1:GDN_CONV, :]
    for back in range(1, GDN_CONV):
        y = y + prev[pl.ds(tail - back, tm), :] * cw[GDN_CONV - 1 - back:GDN_CONV - back, :]
    y = _silu(y)
    is_qk = jp < qk_blocks
    for hh in range(tn // LANES):
        seg = y[:, hh * LANES:(hh + 1) * LANES]
        ss = jnp.sum(seg * seg, axis=-1, keepdims=True)
        o_ref[:, hh * LANES:(hh + 1) * LANES] = seg * jnp.where(
            is_qk, lax.rsqrt(ss + GDN_NORM_EPS), 1.0)

    pre_ref[s % 2, tail:, :] = jnp.dot(xb_ref[...], w_ref[...], preferred_element_type=F32)


def _gdn_qkv(h, w, conv_w, seq, *, tm=512, tn=1024):
    t, d = h.shape
    c = GDN_CONV_DIM
    ncb = c // tn
    nsteps = (t // tm) * ncb
    tail = 8
    kern = functools.partial(_gdn_qkv_kernel, ncb=ncb, tiles_per_seq=seq // tm,
                             qk_blocks=2 * GDN_QK_DIM // tn)

    def cur(s):
        return jnp.minimum(s, nsteps - 1)

    def prv(s):
        return jnp.maximum(s - 1, 0)

    return pl.pallas_call(
        kern,
        grid=(nsteps + 1,),
        in_specs=[pl.BlockSpec((tm, d), lambda s: (cur(s) // ncb, 0)),
                  pl.BlockSpec((d, tn), lambda s: (0, cur(s) % ncb)),
                  pl.BlockSpec((GDN_CONV, tn), lambda s: (0, prv(s) % ncb))],
        out_specs=pl.BlockSpec((tm, tn), lambda s: (prv(s) // ncb, prv(s) % ncb)),
        out_shape=jax.ShapeDtypeStruct((t, c), F32),
        scratch_shapes=[pltpu.VMEM((tm, d), BF16),
                        pltpu.VMEM((2, tm + tail, tn), F32),
                        pltpu.VMEM((ncb, tail, tn), F32)],
        compiler_params=_params("arbitrary"),
        name="gdn_qkv_conv",
    )(h, w, conv_w)


def _lane_pick(x, lane_idx, lane):
    return jnp.sum(jnp.where(lane_idx == lane, x, 0.0), axis=1, keepdims=True)


def _split3(x):
    hi = x.astype(BF16)
    r1 = x - hi.astype(F32)
    mid = r1.astype(BF16)
    lo = (r1 - mid.astype(F32)).astype(BF16)
    return hi.astype(F32), mid.astype(F32), lo.astype(F32)


def _gdn_prep_kernel(q_ref, k_ref, v_ref, gb_ref, u_ref, w_ref, qd_ref, ke_ref, at_ref, *, hps):
    hq0 = pl.program_id(1) * hps
    tt = GDN_TILE
    c = GDN_CHUNK
    scale = GDN_HEAD ** -0.5
    lane = lax.broadcasted_iota(jnp.int32, (tt, LANES), 1)
    ii = lax.broadcasted_iota(jnp.int32, (tt, tt), 0)
    jj = lax.broadcasted_iota(jnp.int32, (tt, tt), 1)

    def same(blk):
        sh = int(math.log2(blk))
        return lax.shift_right_logical(ii, sh) == lax.shift_right_logical(jj, sh)

    same_chunk = same(c)
    lower = same_chunk & (ii >= jj)
    strict = same_chunk & (ii > jj)
    eye = jnp.where(ii == jj, 1.0, 0.0)
    is_chunk_end = jj == (ii | (c - 1))
    gbc = gb_ref[...]
    same_base = same(GDN_BASE)
    heads = [(hl, hv) for hl in range(hps) for hv in range(2)]
    kc, qc, kk, qk = {}, {}, {}, {}
    for hl in range(hps):
        kq_cols = pl.ds(hl * GDN_HEAD, GDN_HEAD)
        kc[hl] = k_ref[:, kq_cols]
        qc[hl] = q_ref[:, kq_cols] * scale
        kbf = kc[hl].astype(BF16)
        kk[hl] = lax.dot_general(kbf, kbf, NT_DIMS, preferred_element_type=F32)
        qk[hl] = lax.dot_general(qc[hl].astype(BF16), kbf, NT_DIMS, preferred_element_type=F32)
    beta, gc, gdiff, lfull, p, tinv = {}, {}, {}, {}, {}, {}
    for hd in heads:
        hl, hv = hd
        head = 2 * (hq0 + hl) + hv
        cols = pl.ds((2 * hl + hv) * GDN_HEAD, GDN_HEAD)
        beta[hd] = _lane_pick(gbc, lane, head)
        gc[hd] = _lane_pick(gbc, lane, GDN_V_HEADS + head)
        hi, mid, lo = _split3(gc[hd])
        a_mat = jnp.where(lane == 0, hi, jnp.where(lane == 1, mid, jnp.where(
            lane == 2, lo, jnp.where(lane < 6, 1.0, 0.0)))).astype(BF16)
        b_mat = jnp.where(lane < 3, 1.0, jnp.where(lane == 3, -hi, jnp.where(
            lane == 4, -mid, jnp.where(lane == 5, -lo, 0.0)))).astype(BF16)
        gdiff[hd] = lax.dot_general(a_mat, b_mat, NT_DIMS, preferred_element_type=F32)
        decay = jnp.where(lower, jnp.exp(jnp.where(lower, gdiff[hd], 0.0)), 0.0)
        lfull[hd] = jnp.where(strict, kk[hl] * beta[hd] * decay, 0.0)
        attn = qk[hl] * decay
        for ch in range(tt // c):
            lt = (ch * c) // LANES
            at_ref[ch * c:(ch + 1) * c, cols] = attn[ch * c:(ch + 1) * c,
                                                     lt * LANES:(lt + 1) * LANES].astype(BF16)
        eg = jnp.exp(gc[hd])
        to_end = jnp.exp(-jnp.sum(jnp.where(is_chunk_end, gdiff[hd], 0.0), axis=1, keepdims=True))
        qd_ref[:, cols] = (qc[hl] * eg).astype(BF16)
        ke_ref[:, cols] = (kc[hl] * to_end).astype(BF16)
        p[hd] = -jnp.where(same_base, lfull[hd], 0.0)
        tinv[hd] = eye + p[hd]
    for _ in range(int(math.log2(GDN_BASE)) - 1):
        for hd in heads:
            pb = p[hd].astype(BF16)
            p[hd] = jnp.dot(pb, pb, preferred_element_type=F32)
        for hd in heads:
            tinv[hd] = tinv[hd] + jnp.dot(tinv[hd].astype(BF16), p[hd].astype(BF16),
                                          preferred_element_type=F32)
    width = GDN_BASE
    while width < c:
        off_mask = same(2 * width) & jnp.logical_not(same(width))
        inner, tb = {}, {}
        for hd in heads:
            tb[hd] = tinv[hd].astype(BF16)
            inner[hd] = jnp.dot(jnp.where(off_mask, lfull[hd], 0.0).astype(BF16), tb[hd],
                                preferred_element_type=F32)
        for hd in heads:
            tinv[hd] = tinv[hd] - jnp.dot(tb[hd], inner[hd].astype(BF16),
                                          preferred_element_type=F32)
        width *= 2
    for hd in heads:
        hl, hv = hd
        cols = pl.ds((2 * hl + hv) * GDN_HEAD, GDN_HEAD)
        rhs = jnp.concatenate([v_ref[:, cols] * beta[hd],
                               kc[hl] * beta[hd] * jnp.exp(gc[hd])], axis=1)
        x = rhs + jnp.dot((tinv[hd] - eye).astype(BF16), rhs.astype(BF16),
                          preferred_element_type=F32)
        u_ref[:, cols] = x[:, :GDN_HEAD]
        w_ref[:, cols] = x[:, GDN_HEAD:].astype(BF16)


def _gdn_prep(qkv, gb, *, hps=2):
    t = qkv.shape[0]
    tt = GDN_TILE
    vd = GDN_V_DIM
    qw = hps * GDN_HEAD
    vw = 2 * qw
    k_blk0 = GDN_QK_DIM // qw
    v_blk0 = 2 * GDN_QK_DIM // vw
    out_spec = pl.BlockSpec((tt, vw), lambda i, h: (i, h))
    kern = functools.partial(_gdn_prep_kernel, hps=hps)
    return pl.pallas_call(
        kern,
        grid=(t // tt, GDN_QK_HEADS // hps),
        in_specs=[pl.BlockSpec((tt, qw), lambda i, h: (i, h)),
                  pl.BlockSpec((tt, qw), lambda i, h: (i, k_blk0 + h)),
                  pl.BlockSpec((tt, vw), lambda i, h: (i, v_blk0 + h)),
                  pl.BlockSpec((tt, LANES), lambda i, h: (i, 0))],
        out_specs=[out_spec] * 5,
        out_shape=[jax.ShapeDtypeStruct((t, vd), F32)] + [jax.ShapeDtypeStruct((t, vd), BF16)] * 4,
        compiler_params=_params("parallel", "parallel"),
        name="gdn_prep",
    )(qkv, qkv, qkv, gb)


def _gdn_rec_kernel(u_ref, w_ref, qd_ref, ke_ref, at_ref, gb_ref, z_ref, nw_ref, o_ref, s_ref,
                    vp_ref, *, hb, cb):
    hg = pl.program_id(1)
    c = GDN_CHUNK

    @pl.when(pl.program_id(2) == 0)
    def _():
        s_ref[...] = jnp.zeros_like(s_ref)
        vp_ref[...] = jnp.zeros_like(vp_ref)

    lane = lax.broadcasted_iota(jnp.int32, (1, LANES), 1)
    nw = nw_ref[...]

    def pair_body(pi, carry):
        for half in range(2):
            r0 = pl.multiple_of(pi * (2 * c) + half * c, c)
            rows = pl.ds(r0, c)
            last = gb_ref[pl.ds(r0 + c - 1, 1), :]
            cols = [pl.ds(hh * GDN_HEAD, GDN_HEAD) for hh in range(hb)]
            state = [s_ref[hh] for hh in range(hb)]
            a1 = [jnp.dot(jnp.concatenate([w_ref[rows, cols[hh]], qd_ref[rows, cols[hh]]], axis=0),
                          state[hh].astype(BF16), preferred_element_type=F32) for hh in range(hb)]
            vb = [(u_ref[rows, cols[hh]] - a1[hh][:c]).astype(BF16) for hh in range(hb)]
            for hh in range(hb):
                vp_ref[hh, half * c:(half + 1) * c, :] = vb[hh]
            o = [a1[hh][c:] + jnp.dot(at_ref[rows, cols[hh]], vp_ref[hh],
                                      preferred_element_type=F32) for hh in range(hb)]
            for hh in range(hb):
                g_last = _lane_pick(last, lane, GDN_V_HEADS + hg * hb + hh)
                s_ref[hh] = state[hh] * jnp.exp(g_last) + lax.dot_general(
                    ke_ref[rows, cols[hh]], vb[hh], TN_DIMS, preferred_element_type=F32)
            for hh in range(hb):
                on = o[hh] * lax.rsqrt(jnp.mean(o[hh] * o[hh], axis=-1, keepdims=True)
                                       + GDN_NORM_EPS) * nw
                o_ref[rows, cols[hh]] = (on * _silu(z_ref[rows, cols[hh]])).astype(o_ref.dtype)
        return carry

    lax.fori_loop(0, cb // 2, pair_body, 0)


def _gdn_rec(u, w, qd, ke, at, gb, z, nw, batch, seq, *, hb=16, cb=4):
    t = u.shape[0]
    tt = cb * GDN_CHUNK
    nsteps = seq // tt
    blk = pl.BlockSpec((tt, hb * GDN_HEAD), lambda b, h, n: (b * nsteps + n, h))
    kern = functools.partial(_gdn_rec_kernel, hb=hb, cb=cb)
    return pl.pallas_call(
        kern,
        grid=(batch, GDN_V_HEADS // hb, nsteps),
        in_specs=[blk, blk, blk, blk, blk,
                  pl.BlockSpec((tt, LANES), lambda b, h, n: (b * nsteps + n, 0)),
                  blk,
                  pl.BlockSpec((1, GDN_HEAD), lambda b, h, n: (0, 0))],
        out_specs=blk,
        out_shape=jax.ShapeDtypeStruct((t, GDN_V_DIM), BF16),
        scratch_shapes=[pltpu.VMEM((hb, GDN_HEAD, GDN_HEAD), F32),
                        pltpu.VMEM((hb, 2 * GDN_CHUNK, GDN_HEAD), BF16)],
        compiler_params=_params("parallel", "parallel", "arbitrary"),
        name="gdn_rec",
    )(u, w, qd, ke, at, gb, z, nw)


def _rope(x, cos_t, sin_lo, sin_hi):
    half = ROPE_DIM // 2
    return (x * cos_t + pltpu.roll(x, LANES - half, axis=1) * sin_lo
            + pltpu.roll(x, half, axis=1) * sin_hi)


def _rope_tables(seq):
    pos = jnp.arange(seq, dtype=F32)
    inv_freq = ROPE_THETA ** (-jnp.arange(0, ROPE_DIM, 2, dtype=F32) / ROPE_DIM)
    ang = pos[:, None] * inv_freq[None, :]
    cos, sin = jnp.cos(ang), jnp.sin(ang)
    half = ROPE_DIM // 2
    pad = LANES - ROPE_DIM
    cos_t = jnp.concatenate([cos, cos, jnp.ones((seq, pad), F32)], axis=1)
    sin_lo = jnp.concatenate([-sin, jnp.zeros((seq, LANES - half), F32)], axis=1)
    sin_hi = jnp.concatenate([jnp.zeros((seq, half), F32), sin, jnp.zeros((seq, pad), F32)], axis=1)
    return cos_t, sin_lo, sin_hi


def _kv_post_kernel(kv_ref, cos_ref, slo_ref, shi_ref, k_ref, vt_ref, km_ref):
    cos_t, slo, shi = cos_ref[...], slo_ref[...], shi_ref[...]
    for hh in range(MOBA_KV_HEADS):
        cols = pl.ds(hh * MOBA_HEAD, MOBA_HEAD)
        kr = _rope(kv_ref[:, cols], cos_t, slo, shi)
        k_ref[:, cols] = kr.astype(BF16)
        km_ref[:, cols] = jnp.mean(kr, axis=0, keepdims=True)
        v = kv_ref[:, pl.ds(MOBA_KV_DIM + hh * MOBA_HEAD, MOBA_HEAD)]
        vt_ref[hh, 0:MOBA_HEAD, :] = v.T.astype(BF16)
        vt_ref[hh, MOBA_HEAD:, :] = jnp.ones((MOBA_ONES_ROWS, MOBA_BLOCK), BF16)


def _kv_post(kv, tables, batch, seq):
    t = kv.shape[0]
    nblk = seq // MOBA_BLOCK
    vrows = MOBA_HEAD + MOBA_ONES_ROWS
    tab = pl.BlockSpec((MOBA_BLOCK, LANES), lambda i: (i % nblk, 0))
    return pl.pallas_call(
        _kv_post_kernel,
        grid=(t // MOBA_BLOCK,),
        in_specs=[pl.BlockSpec((MOBA_BLOCK, 2 * MOBA_KV_DIM), lambda i: (i, 0)), tab, tab, tab],
        out_specs=[pl.BlockSpec((MOBA_BLOCK, MOBA_KV_DIM), lambda i: (i, 0)),
                   pl.BlockSpec((None, MOBA_KV_HEADS, vrows, MOBA_BLOCK),
                                lambda i: (i // nblk, 0, 0, i % nblk)),
                   pl.BlockSpec((None, 1, MOBA_KV_DIM), lambda i: (i, 0, 0))],
        out_shape=[jax.ShapeDtypeStruct((t, MOBA_KV_DIM), BF16),
                   jax.ShapeDtypeStruct((batch, MOBA_KV_HEADS, vrows, seq), BF16),
                   jax.ShapeDtypeStruct((t // MOBA_BLOCK, 1, MOBA_KV_DIM), F32)],
        compiler_params=_params("parallel"),
        name="kv_post",
    )(kv, *tables)


def _moba_kernel(q_ref, cos_ref, slo_ref, shi_ref, k_ref, vt_ref, km_ref, o_ref,
                 qa_ref, m_ref, acc_ref, sa_ref, sb_ref):
    cur = pl.program_id(2)
    blk = MOBA_BLOCK
    nq = MOBA_GROUP * blk
    nblk = k_ref.shape[0] // blk
    kvt = MOBA_KV_GROUP * blk
    scale = MOBA_HEAD ** -0.5 * math.log2(math.e)
    neg_inf = -jnp.inf
    cols = [pl.ds(g * blk, blk) for g in range(MOBA_GROUP)]
    cos_t, slo, shi = cos_ref[...], slo_ref[...], shi_ref[...]
    q_t = jnp.concatenate(
        [(_rope(q_ref[:, g * MOBA_HEAD:(g + 1) * MOBA_HEAD], cos_t, slo, shi) * scale).T
         for g in range(MOBA_GROUP)], axis=1)
    qb = q_t.astype(BF16)
    qa_ref[0:MOBA_HEAD, :] = qb

    nb = km_ref.shape[0]
    nbr = -(-nb // 16) * 16
    km = km_ref[...]
    if nbr > nb:
        km = jnp.concatenate([km, jnp.zeros((nbr - nb, MOBA_HEAD), F32)], axis=0)
    q_lo = (q_t - qb.astype(F32)).astype(BF16)
    kmb = km.astype(BF16)
    km_lo = (km - kmb.astype(F32)).astype(BF16)
    gate = jnp.dot(jnp.concatenate([kmb, km_lo, kmb], axis=1),
                   jnp.concatenate([qb, qb, q_lo], axis=0),
                   preferred_element_type=F32)
    bid = lax.broadcasted_iota(jnp.int32, (nbr, nq), 0).astype(F32)
    gm = jnp.where(bid < cur.astype(F32), gate, neg_inf)
    bias = jnp.full((nbr, nq), MASK_NEG, F32)
    for _ in range(MOBA_TOPK):
        m = jnp.max(gm, axis=0, keepdims=True)
        first = jnp.min(jnp.where(gm == m, bid, float(nbr)), axis=0, keepdims=True)
        pick = (bid == first) & (m > neg_inf)
        bias = jnp.where(pick, 0.0, bias)
        gm = jnp.where(pick, neg_inf, gm)
    qa_ref[MOBA_HEAD:MOBA_HEAD + nbr, :] = bias.astype(BF16)
    if nbr < LANES:
        qa_ref[MOBA_HEAD + nbr:, :] = jnp.full((LANES - nbr, nq), MASK_NEG, BF16)

    lane_k = lax.broadcasted_iota(jnp.int32, (kvt, LANES), 1)
    blk_of_row = lax.shift_right_logical(lax.broadcasted_iota(jnp.int32, (kvt, LANES), 0),
                                         int(math.log2(blk)))

    def group_rows(gi):
        j0 = jnp.minimum(gi * MOBA_KV_GROUP, nblk - MOBA_KV_GROUP)
        return pl.ds(pl.multiple_of(j0 * blk, kvt), kvt)

    def scores(gi, s_ref):
        onehot = jnp.where(lane_k == gi * MOBA_KV_GROUP + blk_of_row, 1.0, 0.0).astype(BF16)
        ka = jnp.concatenate([k_ref[group_rows(gi), :], onehot], axis=1)
        for g in range(MOBA_GROUP):
            s_ref[:, cols[g]] = jnp.dot(ka, qa_ref[:, cols[g]], preferred_element_type=F32)

    def accumulate(gi, s_ref):
        vt = vt_ref[:, group_rows(gi)]
        sj = [s_ref[:, cols[g]] for g in range(MOBA_GROUP)]
        m_old = [m_ref[:, cols[g]] for g in range(MOBA_GROUP)]
        m_new = [jnp.maximum(m_old[g], jnp.max(sj[g], axis=0, keepdims=True))
                 for g in range(MOBA_GROUP)]
        pj = [jnp.exp2(sj[g] - m_new[g]).astype(BF16) for g in range(MOBA_GROUP)]
        pv = [jnp.dot(vt, pj[g], preferred_element_type=F32) for g in range(MOBA_GROUP)]
        for g in range(MOBA_GROUP):
            m_ref[:, cols[g]] = m_new[g]
            acc_ref[:, cols[g]] = jnp.exp2(m_old[g] - m_new[g]) * acc_ref[:, cols[g]] + pv[g]

    scores(0, sa_ref)

    c_own = pl.ds(pl.multiple_of(cur * blk, blk), blk)
    s = jnp.dot(k_ref[c_own, :], qb, preferred_element_type=F32)
    kpos = lax.broadcasted_iota(jnp.int32, (blk, nq), 0)
    qpos = lax.broadcasted_iota(jnp.int32, (blk, nq), 1) % blk
    s = jnp.where(kpos <= qpos, s, neg_inf)
    m0 = jnp.max(s, axis=0, keepdims=True)
    m_ref[...] = m0
    acc_ref[...] = jnp.dot(vt_ref[:, c_own], jnp.exp2(s - m0).astype(BF16),
                           preferred_element_type=F32)

    def pair_body(pi, carry):
        scores(2 * pi + 1, sb_ref)
        accumulate(2 * pi, sa_ref)
        scores(2 * pi + 2, sa_ref)
        accumulate(2 * pi + 1, sb_ref)
        return carry

    lax.fori_loop(0, (cur + 2 * MOBA_KV_GROUP - 1) // (2 * MOBA_KV_GROUP), pair_body, 0)
    out_t = acc_ref[0:MOBA_HEAD, :] / acc_ref[MOBA_HEAD:MOBA_HEAD + 1, :]
    for g in range(MOBA_GROUP):
        o_ref[:, g * MOBA_HEAD:(g + 1) * MOBA_HEAD] = out_t[:, g * blk:(g + 1) * blk].T.astype(
            o_ref.dtype)


def _moba(q, tables, kb, vt, km, batch, seq):
    t = q.shape[0]
    nblk = seq // MOBA_BLOCK
    gw = MOBA_GROUP * MOBA_HEAD
    nq = MOBA_GROUP * MOBA_BLOCK
    vrows = MOBA_HEAD + MOBA_ONES_ROWS
    tab = pl.BlockSpec((MOBA_BLOCK, LANES), lambda b, h, c: (c, 0))
    return pl.pallas_call(
        _moba_kernel,
        grid=(batch, MOBA_KV_HEADS, nblk),
        in_specs=[pl.BlockSpec((MOBA_BLOCK, gw), lambda b, h, c: (b * nblk + c, h)),
                  tab, tab, tab,
                  pl.BlockSpec((seq, MOBA_HEAD), lambda b, h, c: (b, h)),
                  pl.BlockSpec((None, None, vrows, seq), lambda b, h, c: (b, h, 0, 0)),
                  pl.BlockSpec((None, nblk, MOBA_HEAD), lambda b, h, c: (b, 0, h))],
        out_specs=pl.BlockSpec((MOBA_BLOCK, gw), lambda b, h, c: (b * nblk + c, h)),
        out_shape=jax.ShapeDtypeStruct((t, MOBA_Q_HEADS * MOBA_HEAD), BF16),
        scratch_shapes=[pltpu.VMEM((2 * MOBA_HEAD, nq), BF16),
                        pltpu.VMEM((1, nq), F32),
                        pltpu.VMEM((vrows, nq), F32),
                        pltpu.VMEM((MOBA_KV_GROUP * MOBA_BLOCK, nq), F32),
                        pltpu.VMEM((MOBA_KV_GROUP * MOBA_BLOCK, nq), F32)],
        compiler_params=_params("parallel", "parallel", "arbitrary"),
        name="moba_attn",
    )(q, *tables, kb, vt, km)


def _ffn_step(h, w_in, w_out, g, b):
    d = w_in.shape[0]
    return _ffn_ln(h, w_in.reshape(d, -1).astype(BF16), w_out.astype(BF16), g[None, :], b[None, :])


def kernel(x, ln_g, ln_b, w_ffn_in, w_ffn_out, gdn_w_in, gdn_conv_w, gdn_a_log, gdn_dt_bias,
           gdn_norm_w, gdn_w_out, moba_w_kv, moba_w_q, moba_w_out):
    batch, seq, d = x.shape
    h = x.reshape(batch * seq, d)
    tables = _rope_tables(seq)

    h = _ffn_step(h, w_ffn_in[0, 0], w_ffn_out[0, 0], ln_g[0, 0], ln_b[0, 0])
    w_in = gdn_w_in[0].astype(BF16)
    zoff = GDN_CONV_DIM + GDN_V_DIM
    w_ba = jnp.pad(w_in[:, zoff:], ((0, 0), (0, LANES - 2 * GDN_V_HEADS)))
    pad_lo = jnp.zeros((GDN_V_HEADS,), F32)
    pad_hi = jnp.zeros((LANES - 2 * GDN_V_HEADS,), F32)
    alog = jnp.concatenate([pad_lo, gdn_a_log[0].astype(F32), pad_hi])[None, :]
    dtb = jnp.concatenate([pad_lo, gdn_dt_bias[0].astype(F32), pad_hi])[None, :]
    qkv = _gdn_qkv(h, w_in, gdn_conv_w[0], seq)
    z = _mm(h, w_in, col0=GDN_CONV_DIM, n=GDN_V_DIM)
    gb = _gdn_gates(h, w_ba, alog, dtb)
    u, w, qd, ke, at = _gdn_prep(qkv, gb)
    o = _gdn_rec(u, w, qd, ke, at, gb, z, gdn_norm_w[0][None, :].astype(F32), batch, seq)
    h = _mm_res_ln(o, gdn_w_out[0].astype(BF16), h, ln_g[0, 1][None, :], ln_b[0, 1][None, :])
    h = _ffn_step(h, w_ffn_in[0, 1], w_ffn_out[0, 1], ln_g[0, 2], ln_b[0, 2])

    kv = _mm(h, moba_w_kv.astype(BF16))
    kb, vt, km = _kv_post(kv, tables, batch, seq)
    km = km.reshape(batch, seq // MOBA_BLOCK, MOBA_KV_DIM)

    h = _ffn_step(h, w_ffn_in[1, 0], w_ffn_out[1, 0], ln_g[1, 0], ln_b[1, 0])
    q = _mm(h, moba_w_q[0].astype(BF16))
    o = _moba(q, tables, kb, vt, km, batch, seq)
    h = _mm_res_ln(o, moba_w_out[0].astype(BF16), h, ln_g[1, 1][None, :], ln_b[1, 1][None, :])
    h = _ffn_step(h, w_ffn_in[1, 1], w_ffn_out[1, 1], ln_g[1, 2], ln_b[1, 2])
    return h.reshape(batch, seq, d)
```

```python
import functools
import math

import jax
import jax.numpy as jnp
from jax import lax
from jax.experimental import pallas as pl
from jax.experimental.pallas import tpu as pltpu

F32 = jnp.float32
BF16 = jnp.bfloat16

DEPTH = 2
DEEPNORM_ALPHA = (2.0 * DEPTH) ** 0.25
LN_EPS = 1e-5
FFN_HALF = 0.5

GDN_QK_HEADS = 16
GDN_V_HEADS = 32
GDN_HEAD = 128
GDN_CONV = 4
GDN_CHUNK = 64
GDN_TILE = 256
GDN_BASE = 16
GDN_QK_DIM = GDN_QK_HEADS * GDN_HEAD
GDN_V_DIM = GDN_V_HEADS * GDN_HEAD
GDN_CONV_DIM = 2 * GDN_QK_DIM + GDN_V_DIM
GDN_NORM_EPS = 1e-6

MOBA_Q_HEADS = 16
MOBA_KV_HEADS = 4
MOBA_HEAD = 128
MOBA_GROUP = MOBA_Q_HEADS // MOBA_KV_HEADS
MOBA_KV_DIM = MOBA_KV_HEADS * MOBA_HEAD
MOBA_BLOCK = 256
MOBA_TOPK = 3
ROPE_THETA = 500000.0
ROPE_DIM = MOBA_HEAD // 4
MOBA_KV_GROUP = 4
MOBA_ONES_ROWS = 16

LANES = 128
VMEM_LIMIT = 56 * 1024 * 1024
MASK_NEG = -1e30

NT_DIMS = (((1,), (1,)), ((), ()))
TN_DIMS = (((0,), (0,)), ((), ()))


def _params(*sem):
    return pltpu.CompilerParams(dimension_semantics=sem, vmem_limit_bytes=VMEM_LIMIT)


def _layer_norm(y, g, b):
    mu = jnp.mean(y, axis=-1, keepdims=True)
    yc = y - mu
    var = jnp.mean(yc * yc, axis=-1, keepdims=True)
    return yc * lax.rsqrt(var + LN_EPS) * g + b


def _silu(x):
    return x * jax.nn.sigmoid(x)


def _ffn_ln_kernel(x_ref, wg_ref, wu_ref, wo_ref, g_ref, b_ref, o_ref, xb_ref, acc_ref):
    f = pl.program_id(1)

    @pl.when(f == 0)
    def _():
        xb_ref[...] = x_ref[...].astype(BF16)
        acc_ref[...] = jnp.zeros_like(acc_ref)

    xb = xb_ref[...]
    gate = jnp.dot(xb, wg_ref[...], preferred_element_type=F32)
    up = jnp.dot(xb, wu_ref[...], preferred_element_type=F32)
    mid = (_silu(gate) * up).astype(BF16)
    acc_ref[...] += jnp.dot(mid, wo_ref[...], preferred_element_type=F32)

    @pl.when(f == pl.num_programs(1) - 1)
    def _():
        y = DEEPNORM_ALPHA * x_ref[...] + FFN_HALF * acc_ref[...]
        o_ref[...] = _layer_norm(y, g_ref[...], b_ref[...])


def _ffn_ln(h, w_in, wo, g, b, *, tm=512, tf=512):
    t, d = h.shape
    f = wo.shape[0]
    nf = f // tf
    return pl.pallas_call(
        _ffn_ln_kernel,
        grid=(t // tm, nf),
        in_specs=[
            pl.BlockSpec((tm, d), lambda i, j: (i, 0)),
            pl.BlockSpec((d, tf), lambda i, j: (0, j)),
            pl.BlockSpec((d, tf), lambda i, j: (0, nf + j)),
            pl.BlockSpec((tf, d), lambda i, j: (j, 0)),
            pl.BlockSpec((1, d), lambda i, j: (0, 0)),
            pl.BlockSpec((1, d), lambda i, j: (0, 0)),
        ],
        out_specs=pl.BlockSpec((tm, d), lambda i, j: (i, 0)),
        out_shape=jax.ShapeDtypeStruct((t, d), F32),
        scratch_shapes=[pltpu.VMEM((tm, d), BF16), pltpu.VMEM((tm, d), F32)],
        compiler_params=_params("parallel", "arbitrary"),
        name="ffn_ln",
    )(h, w_in, w_in, wo, g, b)


def _mm_kernel(x_ref, w_ref, o_ref, xb_ref):
    @pl.when(pl.program_id(1) == 0)
    def _():
        xb_ref[...] = x_ref[...].astype(BF16)

    o_ref[...] = jnp.dot(xb_ref[...], w_ref[...], preferred_element_type=F32).astype(o_ref.dtype)


def _mm(x, w, *, col0=0, n=None, tm=512, tn=2048, out_dtype=F32):
    t, k = x.shape
    n = w.shape[1] if n is None else n
    tn = min(tn, n)
    blk0 = col0 // tn
    return pl.pallas_call(
        _mm_kernel,
        grid=(t // tm, n // tn),
        in_specs=[pl.BlockSpec((tm, k), lambda i, j: (i, 0)),
                  pl.BlockSpec((k, tn), lambda i, j: (0, blk0 + j))],
        out_specs=pl.BlockSpec((tm, tn), lambda i, j: (i, j)),
        out_shape=jax.ShapeDtypeStruct((t, n), out_dtype),
        scratch_shapes=[pltpu.VMEM((tm, k), BF16)],
        compiler_params=_params("parallel", "arbitrary"),
        name="proj",
    )(x, w)


def _mm_res_ln_kernel(x_ref, w_ref, h_ref, g_ref, b_ref, o_ref, acc_ref):
    k = pl.program_id(1)

    @pl.when(k == 0)
    def _():
        acc_ref[...] = jnp.zeros_like(acc_ref)

    acc_ref[...] += jnp.dot(x_ref[...], w_ref[...], preferred_element_type=F32)

    @pl.when(k == pl.num_programs(1) - 1)
    def _():
        y = DEEPNORM_ALPHA * h_ref[...] + acc_ref[...]
        o_ref[...] = _layer_norm(y, g_ref[...], b_ref[...])


def _mm_res_ln(x, w, h, g, b, *, tm=512, tk=2048):
    t, kdim = x.shape
    d = w.shape[1]
    return pl.pallas_call(
        _mm_res_ln_kernel,
        grid=(t // tm, kdim // tk),
        in_specs=[
            pl.BlockSpec((tm, tk), lambda i, k: (i, k)),
            pl.BlockSpec((tk, d), lambda i, k: (k, 0)),
            pl.BlockSpec((tm, d), lambda i, k: (i, 0)),
            pl.BlockSpec((1, d), lambda i, k: (0, 0)),
            pl.BlockSpec((1, d), lambda i, k: (0, 0)),
        ],
        out_specs=pl.BlockSpec((tm, d), lambda i, k: (i, 0)),
        out_shape=jax.ShapeDtypeStruct((t, d), F32),
        scratch_shapes=[pltpu.VMEM((tm, d), F32)],
        compiler_params=_params("parallel", "arbitrary"),
        name="proj_res_ln",
    )(x, w, h, g, b)


def _gdn_gate_kernel(x_ref, w_ref, alog_ref, dtb_ref, o_ref):
    tm = x_ref.shape[0]
    logits = jnp.dot(x_ref[...].astype(BF16), w_ref[...], preferred_element_type=F32)
    lane = lax.broadcasted_iota(jnp.int32, (tm, LANES), 1)
    row = lax.broadcasted_iota(jnp.int32, (tm, LANES), 0) % GDN_CHUNK
    beta = jax.nn.sigmoid(logits)
    z = logits + dtb_ref[...]
    softplus = jnp.maximum(z, 0.0) + jnp.log1p(jnp.exp(-jnp.abs(z)))
    c = -jnp.exp(alog_ref[...]) * softplus
    shift = 1
    while shift < GDN_CHUNK:
        c = c + jnp.where(row >= shift, pltpu.roll(c, shift, axis=0), 0.0)
        shift *= 2
    o_ref[...] = jnp.where(lane < GDN_V_HEADS, beta, jnp.where(lane < 2 * GDN_V_HEADS, c, 0.0))


def _gdn_gates(h, w_ba, alog, dtb, *, tm=512):
    t, d = h.shape
    return pl.pallas_call(
        _gdn_gate_kernel,
        grid=(t // tm,),
        in_specs=[pl.BlockSpec((tm, d), lambda i: (i, 0)),
                  pl.BlockSpec((d, LANES), lambda i: (0, 0)),
                  pl.BlockSpec((1, LANES), lambda i: (0, 0)),
                  pl.BlockSpec((1, LANES), lambda i: (0, 0))],
        out_specs=pl.BlockSpec((tm, LANES), lambda i: (i, 0)),
        out_shape=jax.ShapeDtypeStruct((t, LANES), F32),
        compiler_params=_params("parallel"),
        name="gdn_gates",
    )(h, w_ba, alog, dtb)


def _gdn_qkv_kernel(x_ref, w_ref, cw_ref, o_ref, xb_ref, pre_ref, carry_ref,
                    *, ncb, tiles_per_seq, qk_blocks):
    s = pl.program_id(0)
    tm = x_ref.shape[0]
    tail = carry_ref.shape[1]
    tn = w_ref.shape[1]

    @pl.when(s == 0)
    def _():
        pre_ref[...] = jnp.zeros_like(pre_ref)
        carry_ref[...] = jnp.zeros_like(carry_ref)

    @pl.when(s % ncb == 0)
    def _():
        xb_ref[...] = x_ref[...].astype(BF16)

    sp = jnp.maximum(s - 1, 0)
    ip, jp = sp // ncb, sp % ncb
    prev = pre_ref.at[(s + 1) % 2]
    prev[0:tail, :] = jnp.where(ip % tiles_per_seq == 0, 0.0, carry_ref[jp])
    cur_rows = prev[tail:, :]
    carry_ref[jp] = cur_rows[tm - tail:, :]
    cw = cw_ref[...]
    y = cur_rows * cw[GDN_CONV - 1:GDN_CONV, :]
    for back in range(1, GDN_CONV):
        y = y + prev[pl.ds(tail - back, tm), :] * cw[GDN_CONV - 1 - back:GDN_CONV - back, :]
    y = _silu(y)
    is_qk = jp < qk_blocks
    for hh in range(tn // LANES):
        seg = y[:, hh * LANES:(hh + 1) * LANES]
        ss = jnp.sum(seg * seg, axis=-1, keepdims=True)
        o_ref[:, hh * LANES:(hh + 1) * LANES] = seg * jnp.where(
            is_qk, lax.rsqrt(ss + GDN_NORM_EPS), 1.0)

    pre_ref[s % 2, tail:, :] = jnp.dot(xb_ref[...], w_ref[...], preferred_element_type=F32)


def _gdn_qkv(h, w, conv_w, seq, *, tm=512, tn=1024):
    t, d = h.shape
    c = GDN_CONV_DIM
    ncb = c // tn
    nsteps = (t // tm) * ncb
    tail = 8
    kern = functools.partial(_gdn_qkv_kernel, ncb=ncb, tiles_per_seq=seq // tm,
                             qk_blocks=2 * GDN_QK_DIM // tn)

    def cur(s):
        return jnp.minimum(s, nsteps - 1)

    def prv(s):
        return jnp.maximum(s - 1, 0)

    return pl.pallas_call(
        kern,
        grid=(nsteps + 1,),
        in_specs=[pl.BlockSpec((tm, d), lambda s: (cur(s) // ncb, 0)),
                  pl.BlockSpec((d, tn), lambda s: (0, cur(s) % ncb)),
                  pl.BlockSpec((GDN_CONV, tn), lambda s: (0, prv(s) % ncb))],
        out_specs=pl.BlockSpec((tm, tn), lambda s: (prv(s) // ncb, prv(s) % ncb)),
        out_shape=jax.ShapeDtypeStruct((t, c), F32),
        scratch_shapes=[pltpu.VMEM((tm, d), BF16),
                        pltpu.VMEM((2, tm + tail, tn), F32),
                        pltpu.VMEM((ncb, tail, tn), F32)],
        compiler_params=_params("arbitrary"),
        name="gdn_qkv_conv",
    )(h, w, conv_w)


def _lane_pick(x, lane_idx, lane):
    return jnp.sum(jnp.where(lane_idx == lane, x, 0.0), axis=1, keepdims=True)


def _split3(x):
    hi = x.astype(BF16)
    r1 = x - hi.astype(F32)
    mid = r1.astype(BF16)
    lo = (r1 - mid.astype(F32)).astype(BF16)
    return hi.astype(F32), mid.astype(F32), lo.astype(F32)


def _gdn_prep_kernel(q_ref, k_ref, v_ref, gb_ref, u_ref, w_ref, qd_ref, ke_ref, at_ref, *, hps):
    hq0 = pl.program_id(1) * hps
    tt = GDN_TILE
    c = GDN_CHUNK
    scale = GDN_HEAD ** -0.5
    lane = lax.broadcasted_iota(jnp.int32, (tt, LANES), 1)
    ii = lax.broadcasted_iota(jnp.int32, (tt, tt), 0)
    jj = lax.broadcasted_iota(jnp.int32, (tt, tt), 1)

    def same(blk):
        sh = int(math.log2(blk))
        return lax.shift_right_logical(ii, sh) == lax.shift_right_logical(jj, sh)

    same_chunk = same(c)
    lower = same_chunk & (ii >= jj)
    strict = same_chunk & (ii > jj)
    eye = jnp.where(ii == jj, 1.0, 0.0)
    is_chunk_end = jj == (ii | (c - 1))
    gbc = gb_ref[...]
    ones_rows = jnp.where(lax.broadcasted_iota(jnp.int32, (8, LANES), 1) < 3, 1.0, 0.0).astype(BF16)
    same_base = same(GDN_BASE)
    heads = [(hl, hv) for hl in range(hps) for hv in range(2)]
    kc, qc, kk, qk = {}, {}, {}, {}
    for hl in range(hps):
        kq_cols = pl.ds(hl * GDN_HEAD, GDN_HEAD)
        kc[hl] = k_ref[:, kq_cols]
        qc[hl] = q_ref[:, kq_cols] * scale
        kbf = kc[hl].astype(BF16)
        kk[hl] = lax.dot_general(kbf, kbf, NT_DIMS, preferred_element_type=F32)
        qk[hl] = lax.dot_general(qc[hl].astype(BF16), kbf, NT_DIMS, preferred_element_type=F32)
    beta, gc, gdiff, lfull, p, tinv = {}, {}, {}, {}, {}, {}
    for hd in heads:
        hl, hv = hd
        head = 2 * (hq0 + hl) + hv
        cols = pl.ds((2 * hl + hv) * GDN_HEAD, GDN_HEAD)
        beta[hd] = _lane_pick(gbc, lane, head)
        gc[hd] = _lane_pick(gbc, lane, GDN_V_HEADS + head)
        hi, mid, lo = _split3(gc[hd])
        b_mat = jnp.where(lane == 0, hi, jnp.where(lane == 1, mid, jnp.where(
            lane == 2, lo, 0.0))).astype(BF16)
        gc_row = lax.dot_general(ones_rows, b_mat, NT_DIMS, preferred_element_type=F32)[0:1, :]
        gdiff[hd] = gc[hd] - gc_row
        decay = jnp.where(lower, jnp.exp(jnp.where(lower, gdiff[hd], 0.0)), 0.0)
        lfull[hd] = jnp.where(strict, kk[hl] * beta[hd] * decay, 0.0)
        attn = qk[hl] * decay
        for ch in range(tt // c):
            lt = (ch * c) // LANES
            at_ref[ch * c:(ch + 1) * c, cols] = attn[ch * c:(ch + 1) * c,
                                                     lt * LANES:(lt + 1) * LANES].astype(BF16)
        eg = jnp.exp(gc[hd])
        to_end = jnp.exp(-jnp.sum(jnp.where(is_chunk_end, gdiff[hd], 0.0), axis=1, keepdims=True))
        qd_ref[:, cols] = (qc[hl] * eg).astype(BF16)
        ke_ref[:, cols] = (kc[hl] * to_end).astype(BF16)
        p[hd] = -jnp.where(same_base, lfull[hd], 0.0)
        tinv[hd] = eye + p[hd]
    for _ in range(int(math.log2(GDN_BASE)) - 1):
        for hd in heads:
            pb = p[hd].astype(BF16)
            p[hd] = jnp.dot(pb, pb, preferred_element_type=F32)
        for hd in heads:
            tinv[hd] = tinv[hd] + jnp.dot(tinv[hd].astype(BF16), p[hd].astype(BF16),
                                          preferred_element_type=F32)
    width = GDN_BASE
    while width < c:
        off_mask = same(2 * width) & jnp.logical_not(same(width))
        inner, tb = {}, {}
        for hd in heads:
            tb[hd] = tinv[hd].astype(BF16)
            inner[hd] = jnp.dot(jnp.where(off_mask, lfull[hd], 0.0).astype(BF16), tb[hd],
                                preferred_element_type=F32)
        for hd in heads:
            tinv[hd] = tinv[hd] - jnp.dot(tb[hd], inner[hd].astype(BF16),
                                          preferred_element_type=F32)
        width *= 2
    for hd in heads:
        hl, hv = hd
        cols = pl.ds((2 * hl + hv) * GDN_HEAD, GDN_HEAD)
        rhs = jnp.concatenate([v_ref[:, cols] * beta[hd],
                               kc[hl] * beta[hd] * jnp.exp(gc[hd])], axis=1)
        x = rhs + jnp.dot((tinv[hd] - eye).astype(BF16), rhs.astype(BF16),
                          preferred_element_type=F32)
        u_ref[:, cols] = x[:, :GDN_HEAD]
        w_ref[:, cols] = x[:, GDN_HEAD:].astype(BF16)


def _gdn_prep(qkv, gb, *, hps=4):
    t = qkv.shape[0]
    tt = GDN_TILE
    vd = GDN_V_DIM
    qw = hps * GDN_HEAD
    vw = 2 * qw
    k_blk0 = GDN_QK_DIM // qw
    v_blk0 = 2 * GDN_QK_DIM // vw
    out_spec = pl.BlockSpec((tt, vw), lambda i, h: (i, h))
    kern = functools.partial(_gdn_prep_kernel, hps=hps)
    return pl.pallas_call(
        kern,
        grid=(t // tt, GDN_QK_HEADS // hps),
        in_specs=[pl.BlockSpec((tt, qw), lambda i, h: (i, h)),
                  pl.BlockSpec((tt, qw), lambda i, h: (i, k_blk0 + h)),
                  pl.BlockSpec((tt, vw), lambda i, h: (i, v_blk0 + h)),
                  pl.BlockSpec((tt, LANES), lambda i, h: (i, 0))],
        out_specs=[out_spec] * 5,
        out_shape=[jax.ShapeDtypeStruct((t, vd), F32)] + [jax.ShapeDtypeStruct((t, vd), BF16)] * 4,
        compiler_params=_params("parallel", "parallel"),
        name="gdn_prep",
    )(qkv, qkv, qkv, gb)


def _gdn_rec_kernel(u_ref, w_ref, qd_ref, ke_ref, at_ref, gb_ref, z_ref, nw_ref, o_ref, s_ref,
                    vp_ref, *, hb, cb):
    hg = pl.program_id(1)
    c = GDN_CHUNK

    @pl.when(pl.program_id(2) == 0)
    def _():
        s_ref[...] = jnp.zeros_like(s_ref)
        vp_ref[...] = jnp.zeros_like(vp_ref)

    lane = lax.broadcasted_iota(jnp.int32, (1, LANES), 1)
    nw = nw_ref[...]

    def pair_body(pi, carry):
        for half in range(2):
            r0 = pl.multiple_of(pi * (2 * c) + half * c, c)
            rows = pl.ds(r0, c)
            last = gb_ref[pl.ds(r0 + c - 1, 1), :]
            cols = [pl.ds(hh * GDN_HEAD, GDN_HEAD) for hh in range(hb)]
            state = [s_ref[hh] for hh in range(hb)]
            a1 = [jnp.dot(jnp.concatenate([w_ref[rows, cols[hh]], qd_ref[rows, cols[hh]]], axis=0),
                          state[hh].astype(BF16), preferred_element_type=F32) for hh in range(hb)]
            vb = [(u_ref[rows, cols[hh]] - a1[hh][:c]).astype(BF16) for hh in range(hb)]
            for hh in range(hb):
                vp_ref[hh, half * c:(half + 1) * c, :] = vb[hh]
            o = [a1[hh][c:] + jnp.dot(at_ref[rows, cols[hh]], vp_ref[hh],
                                      preferred_element_type=F32) for hh in range(hb)]
            for hh in range(hb):
                g_last = _lane_pick(last, lane, GDN_V_HEADS + hg * hb + hh)
                s_ref[hh] = state[hh] * jnp.exp(g_last) + lax.dot_general(
                    ke_ref[rows, cols[hh]], vb[hh], TN_DIMS, preferred_element_type=F32)
            for hh in range(hb):
                on = o[hh] * lax.rsqrt(jnp.mean(o[hh] * o[hh], axis=-1, keepdims=True)
                                       + GDN_NORM_EPS) * nw
                o_ref[rows, cols[hh]] = (on * _silu(z_ref[rows, cols[hh]])).astype(o_ref.dtype)
        return carry

    lax.fori_loop(0, cb // 2, pair_body, 0)


def _gdn_rec(u, w, qd, ke, at, gb, z, nw, batch, seq, *, hb=16, cb=4):
    t = u.shape[0]
    tt = cb * GDN_CHUNK
    nsteps = seq // tt
    blk = pl.BlockSpec((tt, hb * GDN_HEAD), lambda b, h, n: (b * nsteps + n, h))
    kern = functools.partial(_gdn_rec_kernel, hb=hb, cb=cb)
    return pl.pallas_call(
        kern,
        grid=(batch, GDN_V_HEADS // hb, nsteps),
        in_specs=[blk, blk, blk, blk, blk,
                  pl.BlockSpec((tt, LANES), lambda b, h, n: (b * nsteps + n, 0)),
                  blk,
                  pl.BlockSpec((1, GDN_HEAD), lambda b, h, n: (0, 0))],
        out_specs=blk,
        out_shape=jax.ShapeDtypeStruct((t, GDN_V_DIM), BF16),
        scratch_shapes=[pltpu.VMEM((hb, GDN_HEAD, GDN_HEAD), F32),
                        pltpu.VMEM((hb, 2 * GDN_CHUNK, GDN_HEAD), BF16)],
        compiler_params=_params("parallel", "parallel", "arbitrary"),
        name="gdn_rec",
    )(u, w, qd, ke, at, gb, z, nw)


def _rope(x, cos_t, sin_lo, sin_hi):
    half = ROPE_DIM // 2
    return (x * cos_t + pltpu.roll(x, LANES - half, axis=1) * sin_lo
            + pltpu.roll(x, half, axis=1) * sin_hi)


def _rope_tables(seq):
    pos = jnp.arange(seq, dtype=F32)
    inv_freq = ROPE_THETA ** (-jnp.arange(0, ROPE_DIM, 2, dtype=F32) / ROPE_DIM)
    ang = pos[:, None] * inv_freq[None, :]
    cos, sin = jnp.cos(ang), jnp.sin(ang)
    half = ROPE_DIM // 2
    pad = LANES - ROPE_DIM
    cos_t = jnp.concatenate([cos, cos, jnp.ones((seq, pad), F32)], axis=1)
    sin_lo = jnp.concatenate([-sin, jnp.zeros((seq, LANES - half), F32)], axis=1)
    sin_hi = jnp.concatenate([jnp.zeros((seq, half), F32), sin, jnp.zeros((seq, pad), F32)], axis=1)
    return cos_t, sin_lo, sin_hi


def _kv_post_kernel(kv_ref, cos_ref, slo_ref, shi_ref, k_ref, vt_ref, km_ref):
    cos_t, slo, shi = cos_ref[...], slo_ref[...], shi_ref[...]
    for hh in range(MOBA_KV_HEADS):
        cols = pl.ds(hh * MOBA_HEAD, MOBA_HEAD)
        kr = _rope(kv_ref[:, cols], cos_t, slo, shi)
        k_ref[:, cols] = kr.astype(BF16)
        km_ref[:, cols] = jnp.mean(kr, axis=0, keepdims=True)
        v = kv_ref[:, pl.ds(MOBA_KV_DIM + hh * MOBA_HEAD, MOBA_HEAD)]
        vt_ref[hh, 0:MOBA_HEAD, :] = v.T.astype(BF16)
        vt_ref[hh, MOBA_HEAD:, :] = jnp.ones((MOBA_ONES_ROWS, MOBA_BLOCK), BF16)


def _kv_post(kv, tables, batch, seq):
    t = kv.shape[0]
    nblk = seq // MOBA_BLOCK
    vrows = MOBA_HEAD + MOBA_ONES_ROWS
    tab = pl.BlockSpec((MOBA_BLOCK, LANES), lambda i: (i % nblk, 0))
    return pl.pallas_call(
        _kv_post_kernel,
        grid=(t // MOBA_BLOCK,),
        in_specs=[pl.BlockSpec((MOBA_BLOCK, 2 * MOBA_KV_DIM), lambda i: (i, 0)), tab, tab, tab],
        out_specs=[pl.BlockSpec((MOBA_BLOCK, MOBA_KV_DIM), lambda i: (i, 0)),
                   pl.BlockSpec((None, MOBA_KV_HEADS, vrows, MOBA_BLOCK),
                                lambda i: (i // nblk, 0, 0, i % nblk)),
                   pl.BlockSpec((None, 1, MOBA_KV_DIM), lambda i: (i, 0, 0))],
        out_shape=[jax.ShapeDtypeStruct((t, MOBA_KV_DIM), BF16),
                   jax.ShapeDtypeStruct((batch, MOBA_KV_HEADS, vrows, seq), BF16),
                   jax.ShapeDtypeStruct((t // MOBA_BLOCK, 1, MOBA_KV_DIM), F32)],
        compiler_params=_params("parallel"),
        name="kv_post",
    )(kv, *tables)


def _moba_kernel(q_ref, cos_ref, slo_ref, shi_ref, k_ref, vt_ref, km_ref, o_ref,
                 qa_ref, m_ref, acc_ref, sa_ref, sb_ref, ma_ref, mb_ref):
    cur = pl.program_id(2)
    blk = MOBA_BLOCK
    nq = MOBA_GROUP * blk
    nblk = k_ref.shape[0] // blk
    kvt = MOBA_KV_GROUP * blk
    scale = MOBA_HEAD ** -0.5 * math.log2(math.e)
    neg_inf = -jnp.inf
    cols = [pl.ds(g * blk, blk) for g in range(MOBA_GROUP)]
    cos_t, slo, shi = cos_ref[...], slo_ref[...], shi_ref[...]
    q_t = jnp.concatenate(
        [(_rope(q_ref[:, g * MOBA_HEAD:(g + 1) * MOBA_HEAD], cos_t, slo, shi) * scale).T
         for g in range(MOBA_GROUP)], axis=1)
    qb = q_t.astype(BF16)
    qa_ref[0:MOBA_HEAD, :] = qb

    nb = km_ref.shape[0]
    nbr = -(-nb // 16) * 16
    km = km_ref[...]
    if nbr > nb:
        km = jnp.concatenate([km, jnp.zeros((nbr - nb, MOBA_HEAD), F32)], axis=0)
    q_lo = (q_t - qb.astype(F32)).astype(BF16)
    kmb = km.astype(BF16)
    km_lo = (km - kmb.astype(F32)).astype(BF16)
    gate = jnp.dot(jnp.concatenate([kmb, km_lo, kmb], axis=1),
                   jnp.concatenate([qb, qb, q_lo], axis=0),
                   preferred_element_type=F32)
    bid = lax.broadcasted_iota(jnp.int32, (nbr, nq), 0).astype(F32)
    gm = jnp.where(bid < cur.astype(F32), gate, neg_inf)
    bias = jnp.full((nbr, nq), MASK_NEG, F32)
    for _ in range(MOBA_TOPK):
        m = jnp.max(gm, axis=0, keepdims=True)
        first = jnp.min(jnp.where(gm == m, bid, float(nbr)), axis=0, keepdims=True)
        pick = (bid == first) & (m > neg_inf)
        bias = jnp.where(pick, 0.0, bias)
        gm = jnp.where(pick, neg_inf, gm)
    qa_ref[MOBA_HEAD:MOBA_HEAD + nbr, :] = bias.astype(BF16)
    if nbr < LANES:
        qa_ref[MOBA_HEAD + nbr:, :] = jnp.full((LANES - nbr, nq), MASK_NEG, BF16)

    lane_k = lax.broadcasted_iota(jnp.int32, (kvt, LANES), 1)
    blk_of_row = lax.shift_right_logical(lax.broadcasted_iota(jnp.int32, (kvt, LANES), 0),
                                         int(math.log2(blk)))

    def group_rows(gi):
        j0 = jnp.minimum(gi * MOBA_KV_GROUP, nblk - MOBA_KV_GROUP)
        return pl.ds(pl.multiple_of(j0 * blk, kvt), kvt)

    def scores(gi, s_ref, mx_ref):
        onehot = jnp.where(lane_k == gi * MOBA_KV_GROUP + blk_of_row, 1.0, 0.0).astype(BF16)
        ka = jnp.concatenate([k_ref[group_rows(gi), :], onehot], axis=1)
        for g in range(MOBA_GROUP):
            sg = jnp.dot(ka, qa_ref[:, cols[g]], preferred_element_type=F32)
            s_ref[:, cols[g]] = sg
            mx_ref[:, cols[g]] = jnp.max(sg, axis=0, keepdims=True)

    def accumulate(gi, s_ref, mx_ref):
        vt = vt_ref[:, group_rows(gi)]
        sj = [s_ref[:, cols[g]] for g in range(MOBA_GROUP)]
        m_old = [m_ref[:, cols[g]] for g in range(MOBA_GROUP)]
        m_new = [jnp.maximum(m_old[g], mx_ref[:, cols[g]]) for g in range(MOBA_GROUP)]
        pj = [jnp.exp2(sj[g] - m_new[g]).astype(BF16) for g in range(MOBA_GROUP)]
        pv = [jnp.dot(vt, pj[g], preferred_element_type=F32) for g in range(MOBA_GROUP)]
        for g in range(MOBA_GROUP):
            m_ref[:, cols[g]] = m_new[g]
            acc_ref[:, cols[g]] = jnp.exp2(m_old[g] - m_new[g]) * acc_ref[:, cols[g]] + pv[g]

    scores(0, sa_ref, ma_ref)

    c_own = pl.ds(pl.multiple_of(cur * blk, blk), blk)
    s = jnp.dot(k_ref[c_own, :], qb, preferred_element_type=F32)
    kpos = lax.broadcasted_iota(jnp.int32, (blk, nq), 0)
    qpos = lax.broadcasted_iota(jnp.int32, (blk, nq), 1) % blk
    s = jnp.where(kpos <= qpos, s, neg_inf)
    m0 = jnp.max(s, axis=0, keepdims=True)
    m_ref[...] = m0
    acc_ref[...] = jnp.dot(vt_ref[:, c_own], jnp.exp2(s - m0).astype(BF16),
                           preferred_element_type=F32)

    def pair_body(pi, carry):
        scores(2 * pi + 1, sb_ref, mb_ref)
        accumulate(2 * pi, sa_ref, ma_ref)
        scores(2 * pi + 2, sa_ref, ma_ref)
        accumulate(2 * pi + 1, sb_ref, mb_ref)
        return carry

    lax.fori_loop(0, (cur + 2 * MOBA_KV_GROUP - 1) // (2 * MOBA_KV_GROUP), pair_body, 0)
    out_t = acc_ref[0:MOBA_HEAD, :] / acc_ref[MOBA_HEAD:MOBA_HEAD + 1, :]
    for g in range(MOBA_GROUP):
        o_ref[:, g * MOBA_HEAD:(g + 1) * MOBA_HEAD] = out_t[:, g * blk:(g + 1) * blk].T.astype(
            o_ref.dtype)


def _moba(q, tables, kb, vt, km, batch, seq):
    t = q.shape[0]
    nblk = seq // MOBA_BLOCK
    gw = MOBA_GROUP * MOBA_HEAD
    nq = MOBA_GROUP * MOBA_BLOCK
    vrows = MOBA_HEAD + MOBA_ONES_ROWS
    tab = pl.BlockSpec((MOBA_BLOCK, LANES), lambda b, h, c: (c, 0))
    return pl.pallas_call(
        _moba_kernel,
        grid=(batch, MOBA_KV_HEADS, nblk),
        in_specs=[pl.BlockSpec((MOBA_BLOCK, gw), lambda b, h, c: (b * nblk + c, h)),
                  tab, tab, tab,
                  pl.BlockSpec((seq, MOBA_HEAD), lambda b, h, c: (b, h)),
                  pl.BlockSpec((None, None, vrows, seq), lambda b, h, c: (b, h, 0, 0)),
                  pl.BlockSpec((None, nblk, MOBA_HEAD), lambda b, h, c: (b, 0, h))],
        out_specs=pl.BlockSpec((MOBA_BLOCK, gw), lambda b, h, c: (b * nblk + c, h)),
        out_shape=jax.ShapeDtypeStruct((t, MOBA_Q_HEADS * MOBA_HEAD), BF16),
        scratch_shapes=[pltpu.VMEM((2 * MOBA_HEAD, nq), BF16),
                        pltpu.VMEM((1, nq), F32),
                        pltpu.VMEM((vrows, nq), F32),
                        pltpu.VMEM((MOBA_KV_GROUP * MOBA_BLOCK, nq), F32),
                        pltpu.VMEM((MOBA_KV_GROUP * MOBA_BLOCK, nq), F32),
                        pltpu.VMEM((1, nq), F32),
                        pltpu.VMEM((1, nq), F32)],
        compiler_params=_params("parallel", "parallel", "arbitrary"),
        name="moba_attn",
    )(q, *tables, kb, vt, km)


def _ffn_step(h, w_in, w_out, g, b):
    d = w_in.shape[0]
    return _ffn_ln(h, w_in.reshape(d, -1).astype(BF16), w_out.astype(BF16), g[None, :], b[None, :])


def kernel(x, ln_g, ln_b, w_ffn_in, w_ffn_out, gdn_w_in, gdn_conv_w, gdn_a_log, gdn_dt_bias,
           gdn_norm_w, gdn_w_out, moba_w_kv, moba_w_q, moba_w_out):
    batch, seq, d = x.shape
    h = x.reshape(batch * seq, d)
    tables = _rope_tables(seq)

    h = _ffn_step(h, w_ffn_in[0, 0], w_ffn_out[0, 0], ln_g[0, 0], ln_b[0, 0])
    w_in = gdn_w_in[0].astype(BF16)
    zoff = GDN_CONV_DIM + GDN_V_DIM
    w_ba = jnp.pad(w_in[:, zoff:], ((0, 0), (0, LANES - 2 * GDN_V_HEADS)))
    pad_lo = jnp.zeros((GDN_V_HEADS,), F32)
    pad_hi = jnp.zeros((LANES - 2 * GDN_V_HEADS,), F32)
    alog = jnp.concatenate([pad_lo, gdn_a_log[0].astype(F32), pad_hi])[None, :]
    dtb = jnp.concatenate([pad_lo, gdn_dt_bias[0].astype(F32), pad_hi])[None, :]
    qkv = _gdn_qkv(h, w_in, gdn_conv_w[0], seq)
    z = _mm(h, w_in, col0=GDN_CONV_DIM, n=GDN_V_DIM)
    gb = _gdn_gates(h, w_ba, alog, dtb)
    u, w, qd, ke, at = _gdn_prep(qkv, gb)
    o = _gdn_rec(u, w, qd, ke, at, gb, z, gdn_norm_w[0][None, :].astype(F32), batch, seq)
    h = _mm_res_ln(o, gdn_w_out[0].astype(BF16), h, ln_g[0, 1][None, :], ln_b[0, 1][None, :])
    h = _ffn_step(h, w_ffn_in[0, 1], w_ffn_out[0, 1], ln_g[0, 2], ln_b[0, 2])

    kv = _mm(h, moba_w_kv.astype(BF16))
    kb, vt, km = _kv_post(kv, tables, batch, seq)
    km = km.reshape(batch, seq // MOBA_BLOCK, MOBA_KV_DIM)

    h = _ffn_step(h, w_ffn_in[1, 0], w_ffn_out[1, 0], ln_g[1, 0], ln_b[1, 0])
    q = _mm(h, moba_w_q[0].astype(BF16))
    o = _moba(q, tables, kb, vt, km, batch, seq)
    h = _mm_res_ln(o, moba_w_out[0].astype(BF16), h, ln_g[1, 1][None, :], ln_b[1, 1][None, :])
    h = _ffn_step(h, w_ffn_in[1, 1], w_ffn_out[1, 1], ln_g[1, 2], ln_b[1, 2])
    return h.reshape(batch, seq, d)
```

```python
import functools
import math

import jax
import jax.numpy as jnp
from jax import lax
from jax.experimental import pallas as pl
from jax.experimental.pallas import tpu as pltpu

F32 = jnp.float32
BF16 = jnp.bfloat16

DEPTH = 2
DEEPNORM_ALPHA = (2.0 * DEPTH) ** 0.25
LN_EPS = 1e-5
FFN_HALF = 0.5

GDN_QK_HEADS = 16
GDN_V_HEADS = 32
GDN_HEAD = 128
GDN_CONV = 4
GDN_CHUNK = 64
GDN_TILE = 256
GDN_BASE = 16
GDN_QK_DIM = GDN_QK_HEADS * GDN_HEAD
GDN_V_DIM = GDN_V_HEADS * GDN_HEAD
GDN_CONV_DIM = 2 * GDN_QK_DIM + GDN_V_DIM
GDN_NORM_EPS = 1e-6

MOBA_Q_HEADS = 16
MOBA_KV_HEADS = 4
MOBA_HEAD = 128
MOBA_GROUP = MOBA_Q_HEADS // MOBA_KV_HEADS
MOBA_KV_DIM = MOBA_KV_HEADS * MOBA_HEAD
MOBA_BLOCK = 256
MOBA_TOPK = 3
ROPE_THETA = 500000.0
ROPE_DIM = MOBA_HEAD // 4
MOBA_KV_GROUP = 4
MOBA_ONES_ROWS = 16

LANES = 128
VMEM_LIMIT = 56 * 1024 * 1024
MASK_NEG = -1e30

NT_DIMS = (((1,), (1,)), ((), ()))
TN_DIMS = (((0,), (0,)), ((), ()))


def _params(*sem):
    return pltpu.CompilerParams(dimension_semantics=sem, vmem_limit_bytes=VMEM_LIMIT)


def _layer_norm(y, g, b):
    mu = jnp.mean(y, axis=-1, keepdims=True)
    yc = y - mu
    var = jnp.mean(yc * yc, axis=-1, keepdims=True)
    return yc * lax.rsqrt(var + LN_EPS) * g + b


def _silu(x):
    return x * jax.nn.sigmoid(x)


def _ffn_ln_kernel(x_ref, wg_ref, wu_ref, wo_ref, g_ref, b_ref, o_ref, xb_ref, acc_ref):
    f = pl.program_id(1)

    @pl.when(f == 0)
    def _():
        xb_ref[...] = x_ref[...].astype(BF16)
        acc_ref[...] = jnp.zeros_like(acc_ref)

    xb = xb_ref[...]
    gate = jnp.dot(xb, wg_ref[...], preferred_element_type=F32)
    up = jnp.dot(xb, wu_ref[...], preferred_element_type=F32)
    mid = (_silu(gate) * up).astype(BF16)
    acc_ref[...] += jnp.dot(mid, wo_ref[...], preferred_element_type=F32)

    @pl.when(f == pl.num_programs(1) - 1)
    def _():
        y = DEEPNORM_ALPHA * x_ref[...] + FFN_HALF * acc_ref[...]
        o_ref[...] = _layer_norm(y, g_ref[...], b_ref[...])


def _ffn_ln(h, w_in, wo, g, b, *, tm=512, tf=512):
    t, d = h.shape
    f = wo.shape[0]
    nf = f // tf
    return pl.pallas_call(
        _ffn_ln_kernel,
        grid=(t // tm, nf),
        in_specs=[
            pl.BlockSpec((tm, d), lambda i, j: (i, 0)),
            pl.BlockSpec((d, tf), lambda i, j: (0, j)),
            pl.BlockSpec((d, tf), lambda i, j: (0, nf + j)),
            pl.BlockSpec((tf, d), lambda i, j: (j, 0)),
            pl.BlockSpec((1, d), lambda i, j: (0, 0)),
            pl.BlockSpec((1, d), lambda i, j: (0, 0)),
        ],
        out_specs=pl.BlockSpec((tm, d), lambda i, j: (i, 0)),
        out_shape=jax.ShapeDtypeStruct((t, d), F32),
        scratch_shapes=[pltpu.VMEM((tm, d), BF16), pltpu.VMEM((tm, d), F32)],
        compiler_params=_params("parallel", "arbitrary"),
        name="ffn_ln",
    )(h, w_in, w_in, wo, g, b)


def _mm_kernel(x_ref, w_ref, o_ref, xb_ref):
    @pl.when(pl.program_id(1) == 0)
    def _():
        xb_ref[...] = x_ref[...].astype(BF16)

    o_ref[...] = jnp.dot(xb_ref[...], w_ref[...], preferred_element_type=F32).astype(o_ref.dtype)


def _mm(x, w, *, col0=0, n=None, tm=512, tn=2048, out_dtype=F32):
    t, k = x.shape
    n = w.shape[1] if n is None else n
    tn = min(tn, n)
    blk0 = col0 // tn
    return pl.pallas_call(
        _mm_kernel,
        grid=(t // tm, n // tn),
        in_specs=[pl.BlockSpec((tm, k), lambda i, j: (i, 0)),
                  pl.BlockSpec((k, tn), lambda i, j: (0, blk0 + j))],
        out_specs=pl.BlockSpec((tm, tn), lambda i, j: (i, j)),
        out_shape=jax.ShapeDtypeStruct((t, n), out_dtype),
        scratch_shapes=[pltpu.VMEM((tm, k), BF16)],
        compiler_params=_params("parallel", "arbitrary"),
        name="proj",
    )(x, w)


def _mm_res_ln_kernel(x_ref, w_ref, h_ref, g_ref, b_ref, o_ref, acc_ref):
    k = pl.program_id(1)

    @pl.when(k == 0)
    def _():
        acc_ref[...] = jnp.zeros_like(acc_ref)

    acc_ref[...] += jnp.dot(x_ref[...], w_ref[...], preferred_element_type=F32)

    @pl.when(k == pl.num_programs(1) - 1)
    def _():
        y = DEEPNORM_ALPHA * h_ref[...] + acc_ref[...]
        o_ref[...] = _layer_norm(y, g_ref[...], b_ref[...])


def _mm_res_ln(x, w, h, g, b, *, tm=512, tk=2048):
    t, kdim = x.shape
    d = w.shape[1]
    return pl.pallas_call(
        _mm_res_ln_kernel,
        grid=(t // tm, kdim // tk),
        in_specs=[
            pl.BlockSpec((tm, tk), lambda i, k: (i, k)),
            pl.BlockSpec((tk, d), lambda i, k: (k, 0)),
            pl.BlockSpec((tm, d), lambda i, k: (i, 0)),
            pl.BlockSpec((1, d), lambda i, k: (0, 0)),
            pl.BlockSpec((1, d), lambda i, k: (0, 0)),
        ],
        out_specs=pl.BlockSpec((tm, d), lambda i, k: (i, 0)),
        out_shape=jax.ShapeDtypeStruct((t, d), F32),
        scratch_shapes=[pltpu.VMEM((tm, d), F32)],
        compiler_params=_params("parallel", "arbitrary"),
        name="proj_res_ln",
    )(x, w, h, g, b)


def _gdn_gate_kernel(x_ref, w_ref, alog_ref, dtb_ref, o_ref):
    tm = x_ref.shape[0]
    logits = jnp.dot(x_ref[...].astype(BF16), w_ref[...], preferred_element_type=F32)
    lane = lax.broadcasted_iota(jnp.int32, (tm, LANES), 1)
    row = lax.broadcasted_iota(jnp.int32, (tm, LANES), 0) % GDN_CHUNK
    beta = jax.nn.sigmoid(logits)
    z = logits + dtb_ref[...]
    softplus = jnp.maximum(z, 0.0) + jnp.log1p(jnp.exp(-jnp.abs(z)))
    c = -jnp.exp(alog_ref[...]) * softplus
    shift = 1
    while shift < GDN_CHUNK:
        c = c + jnp.where(row >= shift, pltpu.roll(c, shift, axis=0), 0.0)
        shift *= 2
    o_ref[...] = jnp.where(lane < GDN_V_HEADS, beta, jnp.where(lane < 2 * GDN_V_HEADS, c, 0.0))


def _gdn_gates(h, w_ba, alog, dtb, *, tm=512):
    t, d = h.shape
    return pl.pallas_call(
        _gdn_gate_kernel,
        grid=(t // tm,),
        in_specs=[pl.BlockSpec((tm, d), lambda i: (i, 0)),
                  pl.BlockSpec((d, LANES), lambda i: (0, 0)),
                  pl.BlockSpec((1, LANES), lambda i: (0, 0)),
                  pl.BlockSpec((1, LANES), lambda i: (0, 0))],
        out_specs=pl.BlockSpec((tm, LANES), lambda i: (i, 0)),
        out_shape=jax.ShapeDtypeStruct((t, LANES), F32),
        compiler_params=_params("parallel"),
        name="gdn_gates",
    )(h, w_ba, alog, dtb)


def _gdn_qkv_kernel(x_ref, w_ref, cw_ref, o_ref, xb_ref, pre_ref, carry_ref,
                    *, ncb, tiles_per_seq, qk_blocks):
    s = pl.program_id(0)
    tm = x_ref.shape[0]
    tail = carry_ref.shape[1]
    tn = w_ref.shape[1]

    @pl.when(s == 0)
    def _():
        pre_ref[...] = jnp.zeros_like(pre_ref)
        carry_ref[...] = jnp.zeros_like(carry_ref)

    @pl.when(s % ncb == 0)
    def _():
        xb_ref[...] = x_ref[...].astype(BF16)

    sp = jnp.maximum(s - 1, 0)
    ip, jp = sp // ncb, sp % ncb
    prev = pre_ref.at[(s + 1) % 2]
    prev[0:tail, :] = jnp.where(ip % tiles_per_seq == 0, 0.0, carry_ref[jp])
    cur_rows = prev[tail:, :]
    carry_ref[jp] = cur_rows[tm - tail:, :]
    cw = cw_ref[...]
    y = cur_rows * cw[GDN_CONV - 1:GDN_CONV, :]
    for back in range(1, GDN_CONV):
        y = y + prev[pl.ds(tail - back, tm), :] * cw[GDN_CONV - 1 - back:GDN_CONV - back, :]
    y = _silu(y)
    is_qk = jp < qk_blocks
    for hh in range(tn // LANES):
        seg = y[:, hh * LANES:(hh + 1) * LANES]
        ss = jnp.sum(seg * seg, axis=-1, keepdims=True)
        o_ref[:, hh * LANES:(hh + 1) * LANES] = seg * jnp.where(
            is_qk, lax.rsqrt(ss + GDN_NORM_EPS), 1.0)

    pre_ref[s % 2, tail:, :] = jnp.dot(xb_ref[...], w_ref[...], preferred_element_type=F32)


def _gdn_qkv(h, w, conv_w, seq, *, tm=512, tn=1024):
    t, d = h.shape
    c = GDN_CONV_DIM
    ncb = c // tn
    nsteps = (t // tm) * ncb
    tail = 8
    kern = functools.partial(_gdn_qkv_kernel, ncb=ncb, tiles_per_seq=seq // tm,
                             qk_blocks=2 * GDN_QK_DIM // tn)

    def cur(s):
        return jnp.minimum(s, nsteps - 1)

    def prv(s):
        return jnp.maximum(s - 1, 0)

    return pl.pallas_call(
        kern,
        grid=(nsteps + 1,),
        in_specs=[pl.BlockSpec((tm, d), lambda s: (cur(s) // ncb, 0)),
                  pl.BlockSpec((d, tn), lambda s: (0, cur(s) % ncb)),
                  pl.BlockSpec((GDN_CONV, tn), lambda s: (0, prv(s) % ncb))],
        out_specs=pl.BlockSpec((tm, tn), lambda s: (prv(s) // ncb, prv(s) % ncb)),
        out_shape=jax.ShapeDtypeStruct((t, c), F32),
        scratch_shapes=[pltpu.VMEM((tm, d), BF16),
                        pltpu.VMEM((2, tm + tail, tn), F32),
                        pltpu.VMEM((ncb, tail, tn), F32)],
        compiler_params=_params("arbitrary"),
        name="gdn_qkv_conv",
    )(h, w, conv_w)


def _lane_pick(x, lane_idx, lane):
    return jnp.sum(jnp.where(lane_idx == lane, x, 0.0), axis=1, keepdims=True)


def _split3(x):
    hi = x.astype(BF16)
    r1 = x - hi.astype(F32)
    mid = r1.astype(BF16)
    lo = (r1 - mid.astype(F32)).astype(BF16)
    return hi.astype(F32), mid.astype(F32), lo.astype(F32)


def _gdn_core_kernel(q_ref, k_ref, v_ref, gbp_ref, gbr_ref, z_ref, nw_ref, o_ref,
                     s_ref, vp_ref, u_s, w_s, qd_s, ke_s, at_s, *, hps):
    hg = pl.program_id(1)
    n = pl.program_id(2)
    hb = 2 * hps
    tt = GDN_TILE
    c = GDN_CHUNK
    scale = GDN_HEAD ** -0.5

    @pl.when(n == 0)
    def _():
        for ref in (s_ref, vp_ref, u_s, w_s, qd_s, ke_s, at_s):
            ref[...] = jnp.zeros_like(ref)

    lane = lax.broadcasted_iota(jnp.int32, (tt, LANES), 1)
    lane1 = lax.broadcasted_iota(jnp.int32, (1, LANES), 1)
    ii = lax.broadcasted_iota(jnp.int32, (tt, tt), 0)
    jj = lax.broadcasted_iota(jnp.int32, (tt, tt), 1)

    def same(blk):
        sh = int(math.log2(blk))
        return lax.shift_right_logical(ii, sh) == lax.shift_right_logical(jj, sh)

    same_chunk = same(c)
    lower = same_chunk & (ii >= jj)
    strict = same_chunk & (ii > jj)
    eye = jnp.where(ii == jj, 1.0, 0.0)
    is_chunk_end = jj == (ii | (c - 1))
    ones_rows = jnp.where(lax.broadcasted_iota(jnp.int32, (8, LANES), 1) < 3, 1.0, 0.0).astype(BF16)
    same_base = same(GDN_BASE)
    heads = [(hl, hv) for hl in range(hps) for hv in range(2)]
    hcols = [pl.ds(hh * GDN_HEAD, GDN_HEAD) for hh in range(hb)]
    nw = nw_ref[...]

    def body(sw, sr):
        def rec_chunk(ci):
            half = ci % 2
            rows = pl.ds(ci * c, c)
            last = gbr_ref[ci * c + c - 1:ci * c + c, :]
            state = [s_ref[hh] for hh in range(hb)]
            a1 = [jnp.dot(jnp.concatenate([w_s[sr, rows, hcols[hh]], qd_s[sr, rows, hcols[hh]]],
                                          axis=0),
                          state[hh].astype(BF16), preferred_element_type=F32) for hh in range(hb)]
            vb = [(u_s[sr, rows, hcols[hh]] - a1[hh][:c]).astype(BF16) for hh in range(hb)]
            for hh in range(hb):
                vp_ref[hh, half * c:(half + 1) * c, :] = vb[hh]
            o = [a1[hh][c:] + jnp.dot(at_s[sr, rows, hcols[hh]], vp_ref[hh],
                                      preferred_element_type=F32) for hh in range(hb)]
            for hh in range(hb):
                g_last = _lane_pick(last, lane1, GDN_V_HEADS + hg * hb + hh)
                s_ref[hh] = state[hh] * jnp.exp(g_last) + lax.dot_general(
                    ke_s[sr, rows, hcols[hh]], vb[hh], TN_DIMS, preferred_element_type=F32)
            for hh in range(hb):
                on = o[hh] * lax.rsqrt(jnp.mean(o[hh] * o[hh], axis=-1, keepdims=True)
                                       + GDN_NORM_EPS) * nw
                o_ref[rows, hcols[hh]] = (on * _silu(z_ref[rows, hcols[hh]])).astype(o_ref.dtype)

        gbc = gbp_ref[...]
        kc, qc, kk, qk = {}, {}, {}, {}
        for hl in range(hps):
            kq_cols = pl.ds(hl * GDN_HEAD, GDN_HEAD)
            kc[hl] = k_ref[:, kq_cols]
            qc[hl] = q_ref[:, kq_cols] * scale
            kbf = kc[hl].astype(BF16)
            kk[hl] = lax.dot_general(kbf, kbf, NT_DIMS, preferred_element_type=F32)
            qk[hl] = lax.dot_general(qc[hl].astype(BF16), kbf, NT_DIMS,
                                     preferred_element_type=F32)
        beta, gc, gdiff, lfull, p, tinv = {}, {}, {}, {}, {}, {}
        for hd in heads:
            hl, hv = hd
            head = hg * hb + 2 * hl + hv
            cols = hcols[2 * hl + hv]
            beta[hd] = _lane_pick(gbc, lane, head)
            gc[hd] = _lane_pick(gbc, lane, GDN_V_HEADS + head)
            hi, mid, lo = _split3(gc[hd])
            b_mat = jnp.where(lane == 0, hi, jnp.where(lane == 1, mid, jnp.where(
                lane == 2, lo, 0.0))).astype(BF16)
            gc_row = lax.dot_general(ones_rows, b_mat, NT_DIMS, preferred_element_type=F32)[0:1, :]
            gdiff[hd] = gc[hd] - gc_row
            decay = jnp.where(lower, jnp.exp(jnp.where(lower, gdiff[hd], 0.0)), 0.0)
            lfull[hd] = jnp.where(strict, kk[hl] * beta[hd] * decay, 0.0)
            attn = qk[hl] * decay
            for ch in range(tt // c):
                lt = (ch * c) // LANES
                at_s[sw, ch * c:(ch + 1) * c, cols] = attn[ch * c:(ch + 1) * c,
                                                           lt * LANES:(lt + 1) * LANES].astype(BF16)
            eg = jnp.exp(gc[hd])
            to_end = jnp.exp(-jnp.sum(jnp.where(is_chunk_end, gdiff[hd], 0.0), axis=1,
                                      keepdims=True))
            qd_s[sw, :, cols] = (qc[hl] * eg).astype(BF16)
            ke_s[sw, :, cols] = (kc[hl] * to_end).astype(BF16)
            p[hd] = -jnp.where(same_base, lfull[hd], 0.0)
            tinv[hd] = eye + p[hd]
        rec_chunk(0)
        for _ in range(int(math.log2(GDN_BASE)) - 1):
            for hd in heads:
                pb = p[hd].astype(BF16)
                p[hd] = jnp.dot(pb, pb, preferred_element_type=F32)
            for hd in heads:
                tinv[hd] = tinv[hd] + jnp.dot(tinv[hd].astype(BF16), p[hd].astype(BF16),
                                              preferred_element_type=F32)
        rec_chunk(1)
        width = GDN_BASE
        ci = 2
        while width < c:
            off_mask = same(2 * width) & jnp.logical_not(same(width))
            inner, tb = {}, {}
            for hd in heads:
                tb[hd] = tinv[hd].astype(BF16)
                inner[hd] = jnp.dot(jnp.where(off_mask, lfull[hd], 0.0).astype(BF16), tb[hd],
                                    preferred_element_type=F32)
            for hd in heads:
                tinv[hd] = tinv[hd] - jnp.dot(tb[hd], inner[hd].astype(BF16),
                                              preferred_element_type=F32)
            width *= 2
            if ci < tt // c:
                rec_chunk(ci)
                ci += 1
        while ci < tt // c:
            rec_chunk(ci)
            ci += 1
        for hd in heads:
            hl, hv = hd
            cols = hcols[2 * hl + hv]
            rhs = jnp.concatenate([v_ref[:, cols] * beta[hd],
                                   kc[hl] * beta[hd] * jnp.exp(gc[hd])], axis=1)
            x = rhs + jnp.dot((tinv[hd] - eye).astype(BF16), rhs.astype(BF16),
                              preferred_element_type=F32)
            u_s[sw, :, cols] = x[:, :GDN_HEAD]
            w_s[sw, :, cols] = x[:, GDN_HEAD:].astype(BF16)

    @pl.when(n % 2 == 0)
    def _():
        body(0, 1)

    @pl.when(n % 2 == 1)
    def _():
        body(1, 0)


def _gdn_core(qkv, gb, z, nw, batch, seq, *, hps=4):
    t = qkv.shape[0]
    tt = GDN_TILE
    hb = 2 * hps
    nt = seq // tt
    qw = hps * GDN_HEAD
    vw = hb * GDN_HEAD
    k_blk0 = GDN_QK_DIM // qw
    v_blk0 = 2 * GDN_QK_DIM // vw

    def cur(b, n):
        return b * nt + jnp.minimum(n, nt - 1)

    def prv(b, n):
        return b * nt + jnp.maximum(n - 1, 0)

    kern = functools.partial(_gdn_core_kernel, hps=hps)
    return pl.pallas_call(
        kern,
        grid=(batch, GDN_V_HEADS // hb, nt + 1),
        in_specs=[pl.BlockSpec((tt, qw), lambda b, h, n: (cur(b, n), h)),
                  pl.BlockSpec((tt, qw), lambda b, h, n: (cur(b, n), k_blk0 + h)),
                  pl.BlockSpec((tt, vw), lambda b, h, n: (cur(b, n), v_blk0 + h)),
                  pl.BlockSpec((tt, LANES), lambda b, h, n: (cur(b, n), 0)),
                  pl.BlockSpec((tt, LANES), lambda b, h, n: (prv(b, n), 0)),
                  pl.BlockSpec((tt, vw), lambda b, h, n: (prv(b, n), h)),
                  pl.BlockSpec((1, GDN_HEAD), lambda b, h, n: (0, 0))],
        out_specs=pl.BlockSpec((tt, vw), lambda b, h, n: (prv(b, n), h)),
        out_shape=jax.ShapeDtypeStruct((t, GDN_V_DIM), BF16),
        scratch_shapes=[pltpu.VMEM((hb, GDN_HEAD, GDN_HEAD), F32),
                        pltpu.VMEM((hb, 2 * GDN_CHUNK, GDN_HEAD), BF16),
                        pltpu.VMEM((2, tt, vw), F32),
                        pltpu.VMEM((2, tt, vw), BF16),
                        pltpu.VMEM((2, tt, vw), BF16),
                        pltpu.VMEM((2, tt, vw), BF16),
                        pltpu.VMEM((2, tt, vw), BF16)],
        compiler_params=_params("parallel", "parallel", "arbitrary"),
        name="gdn_core",
    )(qkv, qkv, qkv, gb, gb, z, nw)


def _rope(x, cos_t, sin_lo, sin_hi):
    half = ROPE_DIM // 2
    return (x * cos_t + pltpu.roll(x, LANES - half, axis=1) * sin_lo
            + pltpu.roll(x, half, axis=1) * sin_hi)


def _rope_tables(seq):
    pos = jnp.arange(seq, dtype=F32)
    inv_freq = ROPE_THETA ** (-jnp.arange(0, ROPE_DIM, 2, dtype=F32) / ROPE_DIM)
    ang = pos[:, None] * inv_freq[None, :]
    cos, sin = jnp.cos(ang), jnp.sin(ang)
    half = ROPE_DIM // 2
    pad = LANES - ROPE_DIM
    cos_t = jnp.concatenate([cos, cos, jnp.ones((seq, pad), F32)], axis=1)
    sin_lo = jnp.concatenate([-sin, jnp.zeros((seq, LANES - half), F32)], axis=1)
    sin_hi = jnp.concatenate([jnp.zeros((seq, half), F32), sin, jnp.zeros((seq, pad), F32)], axis=1)
    return cos_t, sin_lo, sin_hi


def _kv_post_kernel(kv_ref, cos_ref, slo_ref, shi_ref, k_ref, vt_ref, km_ref):
    cos_t, slo, shi = cos_ref[...], slo_ref[...], shi_ref[...]
    for hh in range(MOBA_KV_HEADS):
        cols = pl.ds(hh * MOBA_HEAD, MOBA_HEAD)
        kr = _rope(kv_ref[:, cols], cos_t, slo, shi)
        k_ref[:, cols] = kr.astype(BF16)
        km_ref[:, cols] = jnp.mean(kr, axis=0, keepdims=True)
        v = kv_ref[:, pl.ds(MOBA_KV_DIM + hh * MOBA_HEAD, MOBA_HEAD)]
        vt_ref[hh, 0:MOBA_HEAD, :] = v.T.astype(BF16)
        vt_ref[hh, MOBA_HEAD:, :] = jnp.ones((MOBA_ONES_ROWS, MOBA_BLOCK), BF16)


def _kv_post(kv, tables, batch, seq):
    t = kv.shape[0]
    nblk = seq // MOBA_BLOCK
    vrows = MOBA_HEAD + MOBA_ONES_ROWS
    tab = pl.BlockSpec((MOBA_BLOCK, LANES), lambda i: (i % nblk, 0))
    return pl.pallas_call(
        _kv_post_kernel,
        grid=(t // MOBA_BLOCK,),
        in_specs=[pl.BlockSpec((MOBA_BLOCK, 2 * MOBA_KV_DIM), lambda i: (i, 0)), tab, tab, tab],
        out_specs=[pl.BlockSpec((MOBA_BLOCK, MOBA_KV_DIM), lambda i: (i, 0)),
                   pl.BlockSpec((None, MOBA_KV_HEADS, vrows, MOBA_BLOCK),
                                lambda i: (i // nblk, 0, 0, i % nblk)),
                   pl.BlockSpec((None, 1, MOBA_KV_DIM), lambda i: (i, 0, 0))],
        out_shape=[jax.ShapeDtypeStruct((t, MOBA_KV_DIM), BF16),
                   jax.ShapeDtypeStruct((batch, MOBA_KV_HEADS, vrows, seq), BF16),
                   jax.ShapeDtypeStruct((t // MOBA_BLOCK, 1, MOBA_KV_DIM), F32)],
        compiler_params=_params("parallel"),
        name="kv_post",
    )(kv, *tables)


def _moba_kernel(q_ref, cos_ref, slo_ref, shi_ref, k_ref, vt_ref, km_ref, o_ref,
                 qa_ref, m_ref, acc_ref, sa_ref, sb_ref, ma_ref, mb_ref):
    cur = pl.program_id(2)
    blk = MOBA_BLOCK
    nq = MOBA_GROUP * blk
    nblk = k_ref.shape[0] // blk
    kvt = MOBA_KV_GROUP * blk
    scale = MOBA_HEAD ** -0.5 * math.log2(math.e)
    neg_inf = -jnp.inf
    cols = [pl.ds(g * blk, blk) for g in range(MOBA_GROUP)]
    cos_t, slo, shi = cos_ref[...], slo_ref[...], shi_ref[...]
    q_t = jnp.concatenate(
        [(_rope(q_ref[:, g * MOBA_HEAD:(g + 1) * MOBA_HEAD], cos_t, slo, shi) * scale).T
         for g in range(MOBA_GROUP)], axis=1)
    qb = q_t.astype(BF16)
    qa_ref[0:MOBA_HEAD, :] = qb

    nb = km_ref.shape[0]
    nbr = -(-nb // 16) * 16
    km = km_ref[...]
    if nbr > nb:
        km = jnp.concatenate([km, jnp.zeros((nbr - nb, MOBA_HEAD), F32)], axis=0)
    q_lo = (q_t - qb.astype(F32)).astype(BF16)
    kmb = km.astype(BF16)
    km_lo = (km - kmb.astype(F32)).astype(BF16)
    gate = jnp.dot(jnp.concatenate([kmb, km_lo, kmb], axis=1),
                   jnp.concatenate([qb, qb, q_lo], axis=0),
                   preferred_element_type=F32)
    bid = lax.broadcasted_iota(jnp.int32, (nbr, nq), 0).astype(F32)
    gm = jnp.where(bid < cur.astype(F32), gate, neg_inf)
    bias = jnp.full((nbr, nq), MASK_NEG, F32)
    for _ in range(MOBA_TOPK):
        m = jnp.max(gm, axis=0, keepdims=True)
        first = jnp.min(jnp.where(gm == m, bid, float(nbr)), axis=0, keepdims=True)
        pick = (bid == first) & (m > neg_inf)
        bias = jnp.where(pick, 0.0, bias)
        gm = jnp.where(pick, neg_inf, gm)
    qa_ref[MOBA_HEAD:MOBA_HEAD + nbr, :] = bias.astype(BF16)
    if nbr < LANES:
        qa_ref[MOBA_HEAD + nbr:, :] = jnp.full((LANES - nbr, nq), MASK_NEG, BF16)

    lane_k = lax.broadcasted_iota(jnp.int32, (kvt, LANES), 1)
    blk_of_row = lax.shift_right_logical(lax.broadcasted_iota(jnp.int32, (kvt, LANES), 0),
                                         int(math.log2(blk)))

    def group_rows(gi):
        j0 = jnp.minimum(gi * MOBA_KV_GROUP, nblk - MOBA_KV_GROUP)
        return pl.ds(pl.multiple_of(j0 * blk, kvt), kvt)

    def scores(gi, s_ref, mx_ref):
        onehot = jnp.where(lane_k == gi * MOBA_KV_GROUP + blk_of_row, 1.0, 0.0).astype(BF16)
        ka = jnp.concatenate([k_ref[group_rows(gi), :], onehot], axis=1)
        for g in range(MOBA_GROUP):
            sg = jnp.dot(ka, qa_ref[:, cols[g]], preferred_element_type=F32)
            s_ref[:, cols[g]] = sg
            mx_ref[:, cols[g]] = jnp.max(sg, axis=0, keepdims=True)

    def accumulate(gi, s_ref, mx_ref):
        vt = vt_ref[:, group_rows(gi)]
        sj = [s_ref[:, cols[g]] for g in range(MOBA_GROUP)]
        m_old = [m_ref[:, cols[g]] for g in range(MOBA_GROUP)]
        m_new = [jnp.maximum(m_old[g], mx_ref[:, cols[g]]) for g in range(MOBA_GROUP)]
        pj = [jnp.exp2(sj[g] - m_new[g]).astype(BF16) for g in range(MOBA_GROUP)]
        pv = [jnp.dot(vt, pj[g], preferred_element_type=F32) for g in range(MOBA_GROUP)]
        for g in range(MOBA_GROUP):
            m_ref[:, cols[g]] = m_new[g]
            acc_ref[:, cols[g]] = jnp.exp2(m_old[g] - m_new[g]) * acc_ref[:, cols[g]] + pv[g]

    scores(0, sa_ref, ma_ref)

    c_own = pl.ds(pl.multiple_of(cur * blk, blk), blk)
    s = jnp.dot(k_ref[c_own, :], qb, preferred_element_type=F32)
    kpos = lax.broadcasted_iota(jnp.int32, (blk, nq), 0)
    qpos = lax.broadcasted_iota(jnp.int32, (blk, nq), 1) % blk
    s = jnp.where(kpos <= qpos, s, neg_inf)
    m0 = jnp.max(s, axis=0, keepdims=True)
    m_ref[...] = m0
    acc_ref[...] = jnp.dot(vt_ref[:, c_own], jnp.exp2(s - m0).astype(BF16),
                           preferred_element_type=F32)

    def pair_body(pi, carry):
        scores(2 * pi + 1, sb_ref, mb_ref)
        accumulate(2 * pi, sa_ref, ma_ref)
        scores(2 * pi + 2, sa_ref, ma_ref)
        accumulate(2 * pi + 1, sb_ref, mb_ref)
        return carry

    lax.fori_loop(0, (cur + 2 * MOBA_KV_GROUP - 1) // (2 * MOBA_KV_GROUP), pair_body, 0)
    out_t = acc_ref[0:MOBA_HEAD, :] / acc_ref[MOBA_HEAD:MOBA_HEAD + 1, :]
    for g in range(MOBA_GROUP):
        o_ref[:, g * MOBA_HEAD:(g + 1) * MOBA_HEAD] = out_t[:, g * blk:(g + 1) * blk].T.astype(
            o_ref.dtype)


def _moba(q, tables, kb, vt, km, batch, seq):
    t = q.shape[0]
    nblk = seq // MOBA_BLOCK
    gw = MOBA_GROUP * MOBA_HEAD
    nq = MOBA_GROUP * MOBA_BLOCK
    vrows = MOBA_HEAD + MOBA_ONES_ROWS
    tab = pl.BlockSpec((MOBA_BLOCK, LANES), lambda b, h, c: (c, 0))
    return pl.pallas_call(
        _moba_kernel,
        grid=(batch, MOBA_KV_HEADS, nblk),
        in_specs=[pl.BlockSpec((MOBA_BLOCK, gw), lambda b, h, c: (b * nblk + c, h)),
                  tab, tab, tab,
                  pl.BlockSpec((seq, MOBA_HEAD), lambda b, h, c: (b, h)),
                  pl.BlockSpec((None, None, vrows, seq), lambda b, h, c: (b, h, 0, 0)),
                  pl.BlockSpec((None, nblk, MOBA_HEAD), lambda b, h, c: (b, 0, h))],
        out_specs=pl.BlockSpec((MOBA_BLOCK, gw), lambda b, h, c: (b * nblk + c, h)),
        out_shape=jax.ShapeDtypeStruct((t, MOBA_Q_HEADS * MOBA_HEAD), BF16),
        scratch_shapes=[pltpu.VMEM((2 * MOBA_HEAD, nq), BF16),
                        pltpu.VMEM((1, nq), F32),
                        pltpu.VMEM((vrows, nq), F32),
                        pltpu.VMEM((MOBA_KV_GROUP * MOBA_BLOCK, nq), F32),
                        pltpu.VMEM((MOBA_KV_GROUP * MOBA_BLOCK, nq), F32),
                        pltpu.VMEM((1, nq), F32),
                        pltpu.VMEM((1, nq), F32)],
        compiler_params=_params("parallel", "parallel", "arbitrary"),
        name="moba_attn",
    )(q, *tables, kb, vt, km)


def _ffn_step(h, w_in, w_out, g, b):
    d = w_in.shape[0]
    return _ffn_ln(h, w_in.reshape(d, -1).astype(BF16), w_out.astype(BF16), g[None, :], b[None, :])


def kernel(x, ln_g, ln_b, w_ffn_in, w_ffn_out, gdn_w_in, gdn_conv_w, gdn_a_log, gdn_dt_bias,
           gdn_norm_w, gdn_w_out, moba_w_kv, moba_w_q, moba_w_out):
    batch, seq, d = x.shape
    h = x.reshape(batch * seq, d)
    tables = _rope_tables(seq)

    h = _ffn_step(h, w_ffn_in[0, 0], w_ffn_out[0, 0], ln_g[0, 0], ln_b[0, 0])
    w_in = gdn_w_in[0].astype(BF16)
    zoff = GDN_CONV_DIM + GDN_V_DIM
    w_ba = jnp.pad(w_in[:, zoff:], ((0, 0), (0, LANES - 2 * GDN_V_HEADS)))
    pad_lo = jnp.zeros((GDN_V_HEADS,), F32)
    pad_hi = jnp.zeros((LANES - 2 * GDN_V_HEADS,), F32)
    alog = jnp.concatenate([pad_lo, gdn_a_log[0].astype(F32), pad_hi])[None, :]
    dtb = jnp.concatenate([pad_lo, gdn_dt_bias[0].astype(F32), pad_hi])[None, :]
    qkv = _gdn_qkv(h, w_in, gdn_conv_w[0], seq)
    z = _mm(h, w_in, col0=GDN_CONV_DIM, n=GDN_V_DIM)
    gb = _gdn_gates(h, w_ba, alog, dtb)
    o = _gdn_core(qkv, gb, z, gdn_norm_w[0][None, :].astype(F32), batch, seq)
    h = _mm_res_ln(o, gdn_w_out[0].astype(BF16), h, ln_g[0, 1][None, :], ln_b[0, 1][None, :])
    h = _ffn_step(h, w_ffn_in[0, 1], w_ffn_out[0, 1], ln_g[0, 2], ln_b[0, 2])

    kv = _mm(h, moba_w_kv.astype(BF16))
    kb, vt, km = _kv_post(kv, tables, batch, seq)
    km = km.reshape(batch, seq // MOBA_BLOCK, MOBA_KV_DIM)

    h = _ffn_step(h, w_ffn_in[1, 0], w_ffn_out[1, 0], ln_g[1, 0], ln_b[1, 0])
    q = _mm(h, moba_w_q[0].astype(BF16))
    o = _moba(q, tables, kb, vt, km, batch, seq)
    h = _mm_res_ln(o, moba_w_out[0].astype(BF16), h, ln_g[1, 1][None, :], ln_b[1, 1][None, :])
    h = _ffn_step(h, w_ffn_in[1, 1], w_ffn_out[1, 1], ln_g[1, 2], ln_b[1, 2])
    return h.reshape(batch, seq, d)
```

```python
import functools
import math

import jax
import jax.numpy as jnp
from jax import lax
from jax.experimental import pallas as pl
from jax.experimental.pallas import tpu as pltpu

F32 = jnp.float32
BF16 = jnp.bfloat16

DEPTH = 2
DEEPNORM_ALPHA = (2.0 * DEPTH) ** 0.25
LN_EPS = 1e-5
FFN_HALF = 0.5

GDN_QK_HEADS = 16
GDN_V_HEADS = 32
GDN_HEAD = 128
GDN_CONV = 4
GDN_CHUNK = 64
GDN_TILE = 256
GDN_BASE = 16
GDN_QK_DIM = GDN_QK_HEADS * GDN_HEAD
GDN_V_DIM = GDN_V_HEADS * GDN_HEAD
GDN_CONV_DIM = 2 * GDN_QK_DIM + GDN_V_DIM
GDN_NORM_EPS = 1e-6

MOBA_Q_HEADS = 16
MOBA_KV_HEADS = 4
MOBA_HEAD = 128
MOBA_GROUP = MOBA_Q_HEADS // MOBA_KV_HEADS
MOBA_KV_DIM = MOBA_KV_HEADS * MOBA_HEAD
MOBA_BLOCK = 256
MOBA_TOPK = 3
ROPE_THETA = 500000.0
ROPE_DIM = MOBA_HEAD // 4
MOBA_KV_GROUP = 4
MOBA_ONES_ROWS = 16

LANES = 128
VMEM_LIMIT = 56 * 1024 * 1024
MASK_NEG = -1e30

NT_DIMS = (((1,), (1,)), ((), ()))
TN_DIMS = (((0,), (0,)), ((), ()))


def _params(*sem):
    return pltpu.CompilerParams(dimension_semantics=sem, vmem_limit_bytes=VMEM_LIMIT)


def _layer_norm(y, g, b):
    mu = jnp.mean(y, axis=-1, keepdims=True)
    yc = y - mu
    var = jnp.mean(yc * yc, axis=-1, keepdims=True)
    return yc * lax.rsqrt(var + LN_EPS) * g + b


def _silu(x):
    return x * jax.nn.sigmoid(x)


def _ffn_ln_kernel(x_ref, wg_ref, wu_ref, wo_ref, g_ref, b_ref, o_ref, xb_ref, acc_ref):
    f = pl.program_id(1)

    @pl.when(f == 0)
    def _():
        xb_ref[...] = x_ref[...].astype(BF16)
        acc_ref[...] = jnp.zeros_like(acc_ref)

    xb = xb_ref[...]
    gate = jnp.dot(xb, wg_ref[...], preferred_element_type=F32)
    up = jnp.dot(xb, wu_ref[...], preferred_element_type=F32)
    mid = (_silu(gate) * up).astype(BF16)
    acc_ref[...] += jnp.dot(mid, wo_ref[...], preferred_element_type=F32)

    @pl.when(f == pl.num_programs(1) - 1)
    def _():
        y = DEEPNORM_ALPHA * x_ref[...] + FFN_HALF * acc_ref[...]
        o_ref[...] = _layer_norm(y, g_ref[...], b_ref[...])


def _ffn_ln(h, w_in, wo, g, b, *, tm=512, tf=512):
    t, d = h.shape
    f = wo.shape[0]
    nf = f // tf
    return pl.pallas_call(
        _ffn_ln_kernel,
        grid=(t // tm, nf),
        in_specs=[
            pl.BlockSpec((tm, d), lambda i, j: (i, 0)),
            pl.BlockSpec((d, tf), lambda i, j: (0, j)),
            pl.BlockSpec((d, tf), lambda i, j: (0, nf + j)),
            pl.BlockSpec((tf, d), lambda i, j: (j, 0)),
            pl.BlockSpec((1, d), lambda i, j: (0, 0)),
            pl.BlockSpec((1, d), lambda i, j: (0, 0)),
        ],
        out_specs=pl.BlockSpec((tm, d), lambda i, j: (i, 0)),
        out_shape=jax.ShapeDtypeStruct((t, d), F32),
        scratch_shapes=[pltpu.VMEM((tm, d), BF16), pltpu.VMEM((tm, d), F32)],
        compiler_params=_params("parallel", "arbitrary"),
        name="ffn_ln",
    )(h, w_in, w_in, wo, g, b)


def _mm_kernel(x_ref, w_ref, o_ref, xb_ref):
    @pl.when(pl.program_id(1) == 0)
    def _():
        xb_ref[...] = x_ref[...].astype(BF16)

    o_ref[...] = jnp.dot(xb_ref[...], w_ref[...], preferred_element_type=F32).astype(o_ref.dtype)


def _mm(x, w, *, col0=0, n=None, tm=512, tn=2048, out_dtype=F32):
    t, k = x.shape
    n = w.shape[1] if n is None else n
    tn = min(tn, n)
    blk0 = col0 // tn
    return pl.pallas_call(
        _mm_kernel,
        grid=(t // tm, n // tn),
        in_specs=[pl.BlockSpec((tm, k), lambda i, j: (i, 0)),
                  pl.BlockSpec((k, tn), lambda i, j: (0, blk0 + j))],
        out_specs=pl.BlockSpec((tm, tn), lambda i, j: (i, j)),
        out_shape=jax.ShapeDtypeStruct((t, n), out_dtype),
        scratch_shapes=[pltpu.VMEM((tm, k), BF16)],
        compiler_params=_params("parallel", "arbitrary"),
        name="proj",
    )(x, w)


def _mm_res_ln_kernel(x_ref, w_ref, h_ref, g_ref, b_ref, o_ref, acc_ref):
    k = pl.program_id(1)

    @pl.when(k == 0)
    def _():
        acc_ref[...] = jnp.zeros_like(acc_ref)

    acc_ref[...] += jnp.dot(x_ref[...], w_ref[...], preferred_element_type=F32)

    @pl.when(k == pl.num_programs(1) - 1)
    def _():
        y = DEEPNORM_ALPHA * h_ref[...] + acc_ref[...]
        o_ref[...] = _layer_norm(y, g_ref[...], b_ref[...])


def _mm_res_ln(x, w, h, g, b, *, tm=512, tk=2048):
    t, kdim = x.shape
    d = w.shape[1]
    return pl.pallas_call(
        _mm_res_ln_kernel,
        grid=(t // tm, kdim // tk),
        in_specs=[
            pl.BlockSpec((tm, tk), lambda i, k: (i, k)),
            pl.BlockSpec((tk, d), lambda i, k: (k, 0)),
            pl.BlockSpec((tm, d), lambda i, k: (i, 0)),
            pl.BlockSpec((1, d), lambda i, k: (0, 0)),
            pl.BlockSpec((1, d), lambda i, k: (0, 0)),
        ],
        out_specs=pl.BlockSpec((tm, d), lambda i, k: (i, 0)),
        out_shape=jax.ShapeDtypeStruct((t, d), F32),
        scratch_shapes=[pltpu.VMEM((tm, d), F32)],
        compiler_params=_params("parallel", "arbitrary"),
        name="proj_res_ln",
    )(x, w, h, g, b)


def _gdn_gate_kernel(x_ref, w_ref, alog_ref, dtb_ref, o_ref):
    tm = x_ref.shape[0]
    logits = jnp.dot(x_ref[...].astype(BF16), w_ref[...], preferred_element_type=F32)
    lane = lax.broadcasted_iota(jnp.int32, (tm, LANES), 1)
    row = lax.broadcasted_iota(jnp.int32, (tm, LANES), 0) % GDN_CHUNK
    beta = jax.nn.sigmoid(logits)
    z = logits + dtb_ref[...]
    softplus = jnp.maximum(z, 0.0) + jnp.log1p(jnp.exp(-jnp.abs(z)))
    c = -jnp.exp(alog_ref[...]) * softplus
    shift = 1
    while shift < GDN_CHUNK:
        c = c + jnp.where(row >= shift, pltpu.roll(c, shift, axis=0), 0.0)
        shift *= 2
    o_ref[...] = jnp.where(lane < GDN_V_HEADS, beta, jnp.where(lane < 2 * GDN_V_HEADS, c, 0.0))


def _gdn_gates(h, w_ba, alog, dtb, *, tm=512):
    t, d = h.shape
    return pl.pallas_call(
        _gdn_gate_kernel,
        grid=(t // tm,),
        in_specs=[pl.BlockSpec((tm, d), lambda i: (i, 0)),
                  pl.BlockSpec((d, LANES), lambda i: (0, 0)),
                  pl.BlockSpec((1, LANES), lambda i: (0, 0)),
                  pl.BlockSpec((1, LANES), lambda i: (0, 0))],
        out_specs=pl.BlockSpec((tm, LANES), lambda i: (i, 0)),
        out_shape=jax.ShapeDtypeStruct((t, LANES), F32),
        compiler_params=_params("parallel"),
        name="gdn_gates",
    )(h, w_ba, alog, dtb)


def _gdn_qkv_kernel(x_ref, w_ref, cw_ref, o_ref, xb_ref, pre_ref, carry_ref,
                    *, ncb, tiles_per_seq, qk_blocks):
    s = pl.program_id(0)
    tm = x_ref.shape[0]
    tail = carry_ref.shape[1]
    tn = w_ref.shape[1]

    @pl.when(s == 0)
    def _():
        pre_ref[...] = jnp.zeros_like(pre_ref)
        carry_ref[...] = jnp.zeros_like(carry_ref)

    @pl.when(s % ncb == 0)
    def _():
        xb_ref[...] = x_ref[...].astype(BF16)

    sp = jnp.maximum(s - 1, 0)
    ip, jp = sp // ncb, sp % ncb
    prev = pre_ref.at[(s + 1) % 2]
    prev[0:tail, :] = jnp.where(ip % tiles_per_seq == 0, 0.0, carry_ref[jp])
    cur_rows = prev[tail:, :]
    carry_ref[jp] = cur_rows[tm - tail:, :]
    cw = cw_ref[...]
    y = cur_rows * cw[GDN_CONV - 1:GDN_CONV, :]
    for back in range(1, GDN_CONV):
        y = y + prev[pl.ds(tail - back, tm), :] * cw[GDN_CONV - 1 - back:GDN_CONV - back, :]
    y = _silu(y)
    is_qk = jp < qk_blocks
    for hh in range(tn // LANES):
        seg = y[:, hh * LANES:(hh + 1) * LANES]
        ss = jnp.sum(seg * seg, axis=-1, keepdims=True)
        o_ref[:, hh * LANES:(hh + 1) * LANES] = seg * jnp.where(
            is_qk, lax.rsqrt(ss + GDN_NORM_EPS), 1.0)

    pre_ref[s % 2, tail:, :] = jnp.dot(xb_ref[...], w_ref[...], preferred_element_type=F32)


def _gdn_qkv(h, w, conv_w, seq, *, tm=512, tn=1024):
    t, d = h.shape
    c = GDN_CONV_DIM
    ncb = c // tn
    nsteps = (t // tm) * ncb
    tail = 8
    kern = functools.partial(_gdn_qkv_kernel, ncb=ncb, tiles_per_seq=seq // tm,
                             qk_blocks=2 * GDN_QK_DIM // tn)

    def cur(s):
        return jnp.minimum(s, nsteps - 1)

    def prv(s):
        return jnp.maximum(s - 1, 0)

    return pl.pallas_call(
        kern,
        grid=(nsteps + 1,),
        in_specs=[pl.BlockSpec((tm, d), lambda s: (cur(s) // ncb, 0)),
                  pl.BlockSpec((d, tn), lambda s: (0, cur(s) % ncb)),
                  pl.BlockSpec((GDN_CONV, tn), lambda s: (0, prv(s) % ncb))],
        out_specs=pl.BlockSpec((tm, tn), lambda s: (prv(s) // ncb, prv(s) % ncb)),
        out_shape=jax.ShapeDtypeStruct((t, c), F32),
        scratch_shapes=[pltpu.VMEM((tm, d), BF16),
                        pltpu.VMEM((2, tm + tail, tn), F32),
                        pltpu.VMEM((ncb, tail, tn), F32)],
        compiler_params=_params("arbitrary"),
        name="gdn_qkv_conv",
    )(h, w, conv_w)


def _lane_pick(x, lane_idx, lane):
    return jnp.sum(jnp.where(lane_idx == lane, x, 0.0), axis=1, keepdims=True)


def _split3(x):
    hi = x.astype(BF16)
    r1 = x - hi.astype(F32)
    mid = r1.astype(BF16)
    lo = (r1 - mid.astype(F32)).astype(BF16)
    return hi.astype(F32), mid.astype(F32), lo.astype(F32)


def _gdn_core_kernel(q_ref, k_ref, v_ref, gbp_ref, gbr_ref, z_ref, nw_ref, o_ref,
                     s_ref, vp_ref, u_s, w_s, qd_s, ke_s, at_s, *, hps):
    hg = pl.program_id(1)
    n = pl.program_id(2)
    hb = 2 * hps
    tt = GDN_TILE
    c = GDN_CHUNK
    scale = GDN_HEAD ** -0.5

    @pl.when(n == 0)
    def _():
        for ref in (s_ref, vp_ref, u_s, w_s, qd_s, ke_s, at_s):
            ref[...] = jnp.zeros_like(ref)

    lane = lax.broadcasted_iota(jnp.int32, (tt, LANES), 1)
    lane1 = lax.broadcasted_iota(jnp.int32, (1, LANES), 1)
    ii = lax.broadcasted_iota(jnp.int32, (tt, tt), 0)
    jj = lax.broadcasted_iota(jnp.int32, (tt, tt), 1)

    def same(blk):
        sh = int(math.log2(blk))
        return lax.shift_right_logical(ii, sh) == lax.shift_right_logical(jj, sh)

    same_chunk = same(c)
    lower = same_chunk & (ii >= jj)
    strict = same_chunk & (ii > jj)
    eye = jnp.where(ii == jj, 1.0, 0.0)
    is_chunk_end = jj == (ii | (c - 1))
    ones_rows = jnp.where(lax.broadcasted_iota(jnp.int32, (8, LANES), 1) < 3, 1.0, 0.0).astype(BF16)
    same_base = same(GDN_BASE)
    heads = [(hl, hv) for hl in range(hps) for hv in range(2)]
    hcols = [pl.ds(hh * GDN_HEAD, GDN_HEAD) for hh in range(hb)]
    nw = nw_ref[...]

    def body(sw, sr):
        def rec_chunk(ci):
            half = ci % 2
            rows = pl.ds(ci * c, c)
            last = gbr_ref[ci * c + c - 1:ci * c + c, :]
            state = [s_ref[hh] for hh in range(hb)]
            a1 = [jnp.dot(jnp.concatenate([w_s[sr, rows, hcols[hh]], qd_s[sr, rows, hcols[hh]]],
                                          axis=0),
                          state[hh].astype(BF16), preferred_element_type=F32) for hh in range(hb)]
            vb = [(u_s[sr, rows, hcols[hh]] - a1[hh][:c]).astype(BF16) for hh in range(hb)]
            for hh in range(hb):
                vp_ref[hh, half * c:(half + 1) * c, :] = vb[hh]
            o = [a1[hh][c:] + jnp.dot(at_s[sr, rows, hcols[hh]], vp_ref[hh],
                                      preferred_element_type=F32) for hh in range(hb)]
            for hh in range(hb):
                g_last = _lane_pick(last, lane1, GDN_V_HEADS + hg * hb + hh)
                s_ref[hh] = state[hh] * jnp.exp(g_last) + lax.dot_general(
                    ke_s[sr, rows, hcols[hh]], vb[hh], TN_DIMS, preferred_element_type=F32)
            for hh in range(hb):
                on = o[hh] * lax.rsqrt(jnp.mean(o[hh] * o[hh], axis=-1, keepdims=True)
                                       + GDN_NORM_EPS) * nw
                o_ref[rows, hcols[hh]] = (on * _silu(z_ref[rows, hcols[hh]])).astype(o_ref.dtype)

        gbc = gbp_ref[...]
        kc, qc, kk, qk = {}, {}, {}, {}
        for hl in range(hps):
            kq_cols = pl.ds(hl * GDN_HEAD, GDN_HEAD)
            kc[hl] = k_ref[:, kq_cols]
            qc[hl] = q_ref[:, kq_cols] * scale
            kbf = kc[hl].astype(BF16)
            kk[hl] = lax.dot_general(kbf, kbf, NT_DIMS, preferred_element_type=F32)
            qk[hl] = lax.dot_general(qc[hl].astype(BF16), kbf, NT_DIMS,
                                     preferred_element_type=F32)
        beta, gc, gdiff, lfull, p, tinv = {}, {}, {}, {}, {}, {}
        for hd in heads:
            hl, hv = hd
            head = hg * hb + 2 * hl + hv
            cols = hcols[2 * hl + hv]
            beta[hd] = _lane_pick(gbc, lane, head)
            gc[hd] = _lane_pick(gbc, lane, GDN_V_HEADS + head)
            hi, mid, lo = _split3(gc[hd])
            b_mat = jnp.where(lane == 0, hi, jnp.where(lane == 1, mid, jnp.where(
                lane == 2, lo, 0.0))).astype(BF16)
            gc_row = lax.dot_general(ones_rows, b_mat, NT_DIMS, preferred_element_type=F32)[0:1, :]
            gdiff[hd] = gc[hd] - gc_row
            decay = jnp.where(lower, jnp.exp(jnp.where(lower, gdiff[hd], 0.0)), 0.0)
            lfull[hd] = jnp.where(strict, kk[hl] * beta[hd] * decay, 0.0)
            attn = qk[hl] * decay
            for ch in range(tt // c):
                lt = (ch * c) // LANES
                at_s[sw, ch * c:(ch + 1) * c, cols] = attn[ch * c:(ch + 1) * c,
                                                           lt * LANES:(lt + 1) * LANES].astype(BF16)
            eg = jnp.exp(gc[hd])
            to_end = jnp.exp(-jnp.sum(jnp.where(is_chunk_end, gdiff[hd], 0.0), axis=1,
                                      keepdims=True))
            qd_s[sw, :, cols] = (qc[hl] * eg).astype(BF16)
            ke_s[sw, :, cols] = (kc[hl] * to_end).astype(BF16)
            p[hd] = -jnp.where(same_base, lfull[hd], 0.0)
            tinv[hd] = eye + p[hd]
        rec_chunk(0)
        for _ in range(int(math.log2(GDN_BASE)) - 1):
            for hd in heads:
                pb = p[hd].astype(BF16)
                p[hd] = jnp.dot(pb, pb, preferred_element_type=F32)
            for hd in heads:
                tinv[hd] = tinv[hd] + jnp.dot(tinv[hd].astype(BF16), p[hd].astype(BF16),
                                              preferred_element_type=F32)
        rec_chunk(1)
        width = GDN_BASE
        ci = 2
        while width < c:
            off_mask = same(2 * width) & jnp.logical_not(same(width))
            inner, tb = {}, {}
            for hd in heads:
                tb[hd] = tinv[hd].astype(BF16)
                inner[hd] = jnp.dot(jnp.where(off_mask, lfull[hd], 0.0).astype(BF16), tb[hd],
                                    preferred_element_type=F32)
            for hd in heads:
                tinv[hd] = tinv[hd] - jnp.dot(tb[hd], inner[hd].astype(BF16),
                                              preferred_element_type=F32)
            width *= 2
            if ci < tt // c:
                rec_chunk(ci)
                ci += 1
        while ci < tt // c:
            rec_chunk(ci)
            ci += 1
        for hd in heads:
            hl, hv = hd
            cols = hcols[2 * hl + hv]
            rhs = jnp.concatenate([v_ref[:, cols] * beta[hd],
                                   kc[hl] * beta[hd] * jnp.exp(gc[hd])], axis=1)
            x = rhs + jnp.dot((tinv[hd] - eye).astype(BF16), rhs.astype(BF16),
                              preferred_element_type=F32)
            u_s[sw, :, cols] = x[:, :GDN_HEAD]
            w_s[sw, :, cols] = x[:, GDN_HEAD:].astype(BF16)

    @pl.when(n % 2 == 0)
    def _():
        body(0, 1)

    @pl.when(n % 2 == 1)
    def _():
        body(1, 0)


def _gdn_core(qkv, gb, z, nw, batch, seq, *, hps=4):
    t = qkv.shape[0]
    tt = GDN_TILE
    hb = 2 * hps
    nt = seq // tt
    qw = hps * GDN_HEAD
    vw = hb * GDN_HEAD
    k_blk0 = GDN_QK_DIM // qw
    v_blk0 = 2 * GDN_QK_DIM // vw

    def cur(b, n):
        return b * nt + jnp.minimum(n, nt - 1)

    def prv(b, n):
        return b * nt + jnp.maximum(n - 1, 0)

    kern = functools.partial(_gdn_core_kernel, hps=hps)
    return pl.pallas_call(
        kern,
        grid=(batch, GDN_V_HEADS // hb, nt + 1),
        in_specs=[pl.BlockSpec((tt, qw), lambda b, h, n: (cur(b, n), h)),
                  pl.BlockSpec((tt, qw), lambda b, h, n: (cur(b, n), k_blk0 + h)),
                  pl.BlockSpec((tt, vw), lambda b, h, n: (cur(b, n), v_blk0 + h)),
                  pl.BlockSpec((tt, LANES), lambda b, h, n: (cur(b, n), 0)),
                  pl.BlockSpec((tt, LANES), lambda b, h, n: (prv(b, n), 0)),
                  pl.BlockSpec((tt, vw), lambda b, h, n: (prv(b, n), h)),
                  pl.BlockSpec((1, GDN_HEAD), lambda b, h, n: (0, 0))],
        out_specs=pl.BlockSpec((tt, vw), lambda b, h, n: (prv(b, n), h)),
        out_shape=jax.ShapeDtypeStruct((t, GDN_V_DIM), BF16),
        scratch_shapes=[pltpu.VMEM((hb, GDN_HEAD, GDN_HEAD), F32),
                        pltpu.VMEM((hb, 2 * GDN_CHUNK, GDN_HEAD), BF16),
                        pltpu.VMEM((2, tt, vw), F32),
                        pltpu.VMEM((2, tt, vw), BF16),
                        pltpu.VMEM((2, tt, vw), BF16),
                        pltpu.VMEM((2, tt, vw), BF16),
                        pltpu.VMEM((2, tt, vw), BF16)],
        compiler_params=_params("parallel", "parallel", "arbitrary"),
        name="gdn_core",
    )(qkv, qkv, qkv, gb, gb, z, nw)


def _rope(x, cos_t, sin_lo, sin_hi):
    half = ROPE_DIM // 2
    return (x * cos_t + pltpu.roll(x, LANES - half, axis=1) * sin_lo
            + pltpu.roll(x, half, axis=1) * sin_hi)


def _rope_tables(seq):
    pos = jnp.arange(seq, dtype=F32)
    inv_freq = ROPE_THETA ** (-jnp.arange(0, ROPE_DIM, 2, dtype=F32) / ROPE_DIM)
    ang = pos[:, None] * inv_freq[None, :]
    cos, sin = jnp.cos(ang), jnp.sin(ang)
    half = ROPE_DIM // 2
    pad = LANES - ROPE_DIM
    cos_t = jnp.concatenate([cos, cos, jnp.ones((seq, pad), F32)], axis=1)
    sin_lo = jnp.concatenate([-sin, jnp.zeros((seq, LANES - half), F32)], axis=1)
    sin_hi = jnp.concatenate([jnp.zeros((seq, half), F32), sin, jnp.zeros((seq, pad), F32)], axis=1)
    return cos_t, sin_lo, sin_hi


def _kv_post_kernel(kv_ref, cos_ref, slo_ref, shi_ref, k_ref, vt_ref, km_ref):
    cos_t, slo, shi = cos_ref[...], slo_ref[...], shi_ref[...]
    for hh in range(MOBA_KV_HEADS):
        cols = pl.ds(hh * MOBA_HEAD, MOBA_HEAD)
        kr = _rope(kv_ref[:, cols], cos_t, slo, shi)
        k_ref[:, cols] = kr.astype(BF16)
        km_ref[:, cols] = jnp.mean(kr, axis=0, keepdims=True)
        v = kv_ref[:, pl.ds(MOBA_KV_DIM + hh * MOBA_HEAD, MOBA_HEAD)]
        vt_ref[hh, 0:MOBA_HEAD, :] = v.T.astype(BF16)
        vt_ref[hh, MOBA_HEAD:, :] = jnp.ones((MOBA_ONES_ROWS, MOBA_BLOCK), BF16)


def _kv_post(kv, tables, batch, seq):
    t = kv.shape[0]
    nblk = seq // MOBA_BLOCK
    vrows = MOBA_HEAD + MOBA_ONES_ROWS
    tab = pl.BlockSpec((MOBA_BLOCK, LANES), lambda i: (i % nblk, 0))
    return pl.pallas_call(
        _kv_post_kernel,
        grid=(t // MOBA_BLOCK,),
        in_specs=[pl.BlockSpec((MOBA_BLOCK, 2 * MOBA_KV_DIM), lambda i: (i, 0)), tab, tab, tab],
        out_specs=[pl.BlockSpec((MOBA_BLOCK, MOBA_KV_DIM), lambda i: (i, 0)),
                   pl.BlockSpec((None, MOBA_KV_HEADS, vrows, MOBA_BLOCK),
                                lambda i: (i // nblk, 0, 0, i % nblk)),
                   pl.BlockSpec((None, 1, MOBA_KV_DIM), lambda i: (i, 0, 0))],
        out_shape=[jax.ShapeDtypeStruct((t, MOBA_KV_DIM), BF16),
                   jax.ShapeDtypeStruct((batch, MOBA_KV_HEADS, vrows, seq), BF16),
                   jax.ShapeDtypeStruct((t // MOBA_BLOCK, 1, MOBA_KV_DIM), F32)],
        compiler_params=_params("parallel"),
        name="kv_post",
    )(kv, *tables)


def _moba_kernel(q_ref, cos_ref, slo_ref, shi_ref, k_ref, vt_ref, km_ref, o_ref,
                 qa_ref, m_ref, acc_ref, sa_ref, sb_ref, ma_ref, mb_ref):
    cur = pl.program_id(2)
    blk = MOBA_BLOCK
    nq = MOBA_GROUP * blk
    nblk = k_ref.shape[0] // blk
    kvt = MOBA_KV_GROUP * blk
    scale = MOBA_HEAD ** -0.5 * math.log2(math.e)
    neg_inf = -jnp.inf
    cols = [pl.ds(g * blk, blk) for g in range(MOBA_GROUP)]
    cos_t, slo, shi = cos_ref[...], slo_ref[...], shi_ref[...]
    q_t = jnp.concatenate(
        [(_rope(q_ref[:, g * MOBA_HEAD:(g + 1) * MOBA_HEAD], cos_t, slo, shi) * scale).T
         for g in range(MOBA_GROUP)], axis=1)
    qb = q_t.astype(BF16)
    qa_ref[0:MOBA_HEAD, :] = qb

    nb = km_ref.shape[0]
    nbr = -(-nb // 16) * 16
    km = km_ref[...]
    if nbr > nb:
        km = jnp.concatenate([km, jnp.zeros((nbr - nb, MOBA_HEAD), F32)], axis=0)
    q_lo = (q_t - qb.astype(F32)).astype(BF16)
    kmb = km.astype(BF16)
    km_lo = (km - kmb.astype(F32)).astype(BF16)
    gate = jnp.dot(jnp.concatenate([kmb, km_lo, kmb], axis=1),
                   jnp.concatenate([qb, qb, q_lo], axis=0),
                   preferred_element_type=F32)
    bid = lax.broadcasted_iota(jnp.int32, (nbr, nq), 0).astype(F32)
    gm = jnp.where(bid < cur.astype(F32), gate, neg_inf)
    bias = jnp.full((nbr, nq), MASK_NEG, F32)
    for _ in range(MOBA_TOPK):
        m = jnp.max(gm, axis=0, keepdims=True)
        first = jnp.min(jnp.where(gm == m, bid, float(nbr)), axis=0, keepdims=True)
        pick = (bid == first) & (m > neg_inf)
        bias = jnp.where(pick, 0.0, bias)
        gm = jnp.where(pick, neg_inf, gm)
    qa_ref[MOBA_HEAD:MOBA_HEAD + nbr, :] = bias.astype(BF16)
    if nbr < LANES:
        qa_ref[MOBA_HEAD + nbr:, :] = jnp.full((LANES - nbr, nq), MASK_NEG, BF16)

    lane_k = lax.broadcasted_iota(jnp.int32, (kvt, LANES), 1)
    blk_of_row = lax.shift_right_logical(lax.broadcasted_iota(jnp.int32, (kvt, LANES), 0),
                                         int(math.log2(blk)))

    def group_rows(gi):
        j0 = jnp.minimum(gi * MOBA_KV_GROUP, nblk - MOBA_KV_GROUP)
        return pl.ds(pl.multiple_of(j0 * blk, kvt), kvt)

    def scores(gi, s_ref, mx_ref):
        onehot = jnp.where(lane_k == gi * MOBA_KV_GROUP + blk_of_row, 1.0, 0.0).astype(BF16)
        ka = jnp.concatenate([k_ref[group_rows(gi), :], onehot], axis=1)
        rk = kvt // 2
        for g in range(MOBA_GROUP):
            parts = [jnp.dot(ka[r * rk:(r + 1) * rk], qa_ref[:, cols[g]],
                             preferred_element_type=F32) for r in range(2)]
            mx = None
            for r in range(2):
                s_ref[r * rk:(r + 1) * rk, cols[g]] = parts[r]
                pm = jnp.max(parts[r], axis=0, keepdims=True)
                mx = pm if mx is None else jnp.maximum(mx, pm)
            mx_ref[:, cols[g]] = mx

    def accumulate(gi, s_ref, mx_ref):
        vt = vt_ref[:, group_rows(gi)]
        sj = [s_ref[:, cols[g]] for g in range(MOBA_GROUP)]
        m_old = [m_ref[:, cols[g]] for g in range(MOBA_GROUP)]
        m_new = [jnp.maximum(m_old[g], mx_ref[:, cols[g]]) for g in range(MOBA_GROUP)]
        pj = [jnp.exp2(sj[g] - m_new[g]).astype(BF16) for g in range(MOBA_GROUP)]
        pv = [jnp.dot(vt, pj[g], preferred_element_type=F32) for g in range(MOBA_GROUP)]
        for g in range(MOBA_GROUP):
            m_ref[:, cols[g]] = m_new[g]
            acc_ref[:, cols[g]] = jnp.exp2(m_old[g] - m_new[g]) * acc_ref[:, cols[g]] + pv[g]

    scores(0, sa_ref, ma_ref)

    c_own = pl.ds(pl.multiple_of(cur * blk, blk), blk)
    s = jnp.dot(k_ref[c_own, :], qb, preferred_element_type=F32)
    kpos = lax.broadcasted_iota(jnp.int32, (blk, nq), 0)
    qpos = lax.broadcasted_iota(jnp.int32, (blk, nq), 1) % blk
    s = jnp.where(kpos <= qpos, s, neg_inf)
    m0 = jnp.max(s, axis=0, keepdims=True)
    m_ref[...] = m0
    acc_ref[...] = jnp.dot(vt_ref[:, c_own], jnp.exp2(s - m0).astype(BF16),
                           preferred_element_type=F32)

    def pair_body(pi, carry):
        scores(2 * pi + 1, sb_ref, mb_ref)
        accumulate(2 * pi, sa_ref, ma_ref)
        scores(2 * pi + 2, sa_ref, ma_ref)
        accumulate(2 * pi + 1, sb_ref, mb_ref)
        return carry

    lax.fori_loop(0, (cur + 2 * MOBA_KV_GROUP - 1) // (2 * MOBA_KV_GROUP), pair_body, 0)
    out_t = acc_ref[0:MOBA_HEAD, :] / acc_ref[MOBA_HEAD:MOBA_HEAD + 1, :]
    for g in range(MOBA_GROUP):
        o_ref[:, g * MOBA_HEAD:(g + 1) * MOBA_HEAD] = out_t[:, g * blk:(g + 1) * blk].T.astype(
            o_ref.dtype)


def _moba(q, tables, kb, vt, km, batch, seq):
    t = q.shape[0]
    nblk = seq // MOBA_BLOCK
    gw = MOBA_GROUP * MOBA_HEAD
    nq = MOBA_GROUP * MOBA_BLOCK
    vrows = MOBA_HEAD + MOBA_ONES_ROWS
    tab = pl.BlockSpec((MOBA_BLOCK, LANES), lambda b, h, c: (c, 0))
    return pl.pallas_call(
        _moba_kernel,
        grid=(batch, MOBA_KV_HEADS, nblk),
        in_specs=[pl.BlockSpec((MOBA_BLOCK, gw), lambda b, h, c: (b * nblk + c, h)),
                  tab, tab, tab,
                  pl.BlockSpec((seq, MOBA_HEAD), lambda b, h, c: (b, h)),
                  pl.BlockSpec((None, None, vrows, seq), lambda b, h, c: (b, h, 0, 0)),
                  pl.BlockSpec((None, nblk, MOBA_HEAD), lambda b, h, c: (b, 0, h))],
        out_specs=pl.BlockSpec((MOBA_BLOCK, gw), lambda b, h, c: (b * nblk + c, h)),
        out_shape=jax.ShapeDtypeStruct((t, MOBA_Q_HEADS * MOBA_HEAD), BF16),
        scratch_shapes=[pltpu.VMEM((2 * MOBA_HEAD, nq), BF16),
                        pltpu.VMEM((1, nq), F32),
                        pltpu.VMEM((vrows, nq), F32),
                        pltpu.VMEM((MOBA_KV_GROUP * MOBA_BLOCK, nq), F32),
                        pltpu.VMEM((MOBA_KV_GROUP * MOBA_BLOCK, nq), F32),
                        pltpu.VMEM((1, nq), F32),
                        pltpu.VMEM((1, nq), F32)],
        compiler_params=_params("parallel", "parallel", "arbitrary"),
        name="moba_attn",
    )(q, *tables, kb, vt, km)


def _ffn_step(h, w_in, w_out, g, b):
    d = w_in.shape[0]
    return _ffn_ln(h, w_in.reshape(d, -1).astype(BF16), w_out.astype(BF16), g[None, :], b[None, :])


def kernel(x, ln_g, ln_b, w_ffn_in, w_ffn_out, gdn_w_in, gdn_conv_w, gdn_a_log, gdn_dt_bias,
           gdn_norm_w, gdn_w_out, moba_w_kv, moba_w_q, moba_w_out):
    batch, seq, d = x.shape
    h = x.reshape(batch * seq, d)
    tables = _rope_tables(seq)

    h = _ffn_step(h, w_ffn_in[0, 0], w_ffn_out[0, 0], ln_g[0, 0], ln_b[0, 0])
    w_in = gdn_w_in[0].astype(BF16)
    zoff = GDN_CONV_DIM + GDN_V_DIM
    w_ba = jnp.pad(w_in[:, zoff:], ((0, 0), (0, LANES - 2 * GDN_V_HEADS)))
    pad_lo = jnp.zeros((GDN_V_HEADS,), F32)
    pad_hi = jnp.zeros((LANES - 2 * GDN_V_HEADS,), F32)
    alog = jnp.concatenate([pad_lo, gdn_a_log[0].astype(F32), pad_hi])[None, :]
    dtb = jnp.concatenate([pad_lo, gdn_dt_bias[0].astype(F32), pad_hi])[None, :]
    qkv = _gdn_qkv(h, w_in, gdn_conv_w[0], seq)
    z = _mm(h, w_in, col0=GDN_CONV_DIM, n=GDN_V_DIM)
    gb = _gdn_gates(h, w_ba, alog, dtb)
    o = _gdn_core(qkv, gb, z, gdn_norm_w[0][None, :].astype(F32), batch, seq)
    h = _mm_res_ln(o, gdn_w_out[0].astype(BF16), h, ln_g[0, 1][None, :], ln_b[0, 1][None, :])
    h = _ffn_step(h, w_ffn_in[0, 1], w_ffn_out[0, 1], ln_g[0, 2], ln_b[0, 2])

    kv = _mm(h, moba_w_kv.astype(BF16))
    kb, vt, km = _kv_post(kv, tables, batch, seq)
    km = km.reshape(batch, seq // MOBA_BLOCK, MOBA_KV_DIM)

    h = _ffn_step(h, w_ffn_in[1, 0], w_ffn_out[1, 0], ln_g[1, 0], ln_b[1, 0])
    q = _mm(h, moba_w_q[0].astype(BF16))
    o = _moba(q, tables, kb, vt, km, batch, seq)
    h = _mm_res_ln(o, moba_w_out[0].astype(BF16), h, ln_g[1, 1][None, :], ln_b[1, 1][None, :])
    h = _ffn_step(h, w_ffn_in[1, 1], w_ffn_out[1, 1], ln_g[1, 2], ln_b[1, 2])
    return h.reshape(batch, seq, d)
```
